```python
import jax, jax.numpy as jnp
from jax import lax
import numpy as np

D_MODEL = 2048
BATCH = 4
SEQ = 4096
DEPTH = 4

CHUNK = 64
N_META = 16
N_MIXERS = 2
N_MLA = (DEPTH + N_MIXERS - 1) // N_MIXERS
N_LRU = DEPTH // N_MIXERS

MLA_HEADS = 16
Q_LORA = 512
KV_LORA = 512
QK_NOPE = 128
QK_ROPE = 64
V_HEAD = 128
ROPE_THETA = 10000.0
Q_BLOCK = 128

D_RNN = D_MODEL
RNN_BLOCKS = 16
RNN_BW = D_RNN // RNN_BLOCKS
CONV_W = 4
LRU_C = 8.0

D_FF = -(-8 * D_MODEL // (3 * 256)) * 256

RMS_EPS = 1e-6
NEG_BIG = -1e30

kernel_name = 'hybrid_mla_rglru_streaming_trunk'


def rms_norm(x, g):
    xf = x.astype(jnp.float32)
    y = xf * lax.rsqrt(jnp.mean(xf * xf, axis=-1, keepdims=True) + RMS_EPS)
    return (y * g.astype(jnp.float32)).astype(x.dtype)


def chunk_ids(n):
    pos = jnp.arange(n)
    return jnp.where(pos < N_META, 0, 1 + (pos - N_META) // CHUNK)


def apply_rope(x, cos, sin):
    xf = x.astype(jnp.float32)
    x1, x2 = jnp.split(xf, 2, axis=-1)
    out = jnp.concatenate([x1 * cos - x2 * sin, x2 * cos + x1 * sin], axis=-1)
    return out.astype(x.dtype)


def mla_mixer(h, w_in, q_norm, kv_norm, w_uq, w_ukv, w_o):
    B, T, _ = h.shape
    proj = h @ w_in
    c_q, c_kv, k_rope = jnp.split(proj, [Q_LORA, Q_LORA + KV_LORA], axis=-1)
    c_q = rms_norm(c_q, q_norm)
    c_kv = rms_norm(c_kv, kv_norm)
    q = (c_q @ w_uq).reshape(B, T, MLA_HEADS, QK_NOPE + QK_ROPE)
    q_nope, q_rope = jnp.split(q, [QK_NOPE], axis=-1)
    kv = (c_kv @ w_ukv).reshape(B, T, MLA_HEADS, QK_NOPE + V_HEAD)
    k_nope, v = jnp.split(kv, [QK_NOPE], axis=-1)

    pos = jnp.arange(T, dtype=jnp.float32)
    inv_freq = ROPE_THETA ** (-jnp.arange(0, QK_ROPE, 2, dtype=jnp.float32) / QK_ROPE)
    ang = pos[:, None] * inv_freq[None, :]
    cos, sin = jnp.cos(ang), jnp.sin(ang)
    q_rope = apply_rope(q_rope, cos[:, None, :], sin[:, None, :])
    k_rope = apply_rope(k_rope, cos, sin)

    scale = (QK_NOPE + QK_ROPE) ** -0.5
    n_blk = -(-T // Q_BLOCK)
    t_pad = n_blk * Q_BLOCK
    pad = ((0, 0), (0, t_pad - T), (0, 0), (0, 0))
    q_nope = jnp.pad(q_nope, pad)
    q_rope = jnp.pad(q_rope, pad)
    k_chunk = chunk_ids(T)
    q_chunk = chunk_ids(t_pad)

    def attend_block(i):
        s = i * Q_BLOCK
        qn = lax.dynamic_slice_in_dim(q_nope, s, Q_BLOCK, axis=1)
        qr = lax.dynamic_slice_in_dim(q_rope, s, Q_BLOCK, axis=1)
        qc = lax.dynamic_slice_in_dim(q_chunk, s, Q_BLOCK)
        scores = (jnp.einsum('bqhd,bkhd->bhqk', qn, k_nope)
                  + jnp.einsum('bqhd,bkd->bhqk', qr, k_rope)).astype(jnp.float32) * scale
        mask = k_chunk[None, :] <= qc[:, None]
        scores = jnp.where(mask[None, None], scores, NEG_BIG)
        p = jax.nn.softmax(scores, axis=-1).astype(v.dtype)
        return jnp.einsum('bhqk,bkhd->bqhd', p, v)

    out = lax.map(attend_block, jnp.arange(n_blk))
    out = jnp.moveaxis(out, 0, 1).reshape(B, t_pad, MLA_HEADS * V_HEAD)[:, :T]
    return out @ w_o


def rglru_mixer(h, w_in, conv_w, conv_b, w_ga, b_ga, w_gx, b_gx, lam, w_o):
    B, T, _ = h.shape
    xb, yb = jnp.split(h @ w_in, 2, axis=-1)
    yb = jax.nn.gelu(yb, approximate=True)
    xb = lax.conv_general_dilated(
        xb, conv_w[:, None, :], window_strides=(1,), padding=[(CONV_W - 1, 0)],
        dimension_numbers=('NWC', 'WIO', 'NWC'), feature_group_count=D_RNN) + conv_b
    xg = xb.reshape(B, T, RNN_BLOCKS, RNN_BW)
    r = jax.nn.sigmoid(jnp.einsum('btnc,ncd->btnd', xg, w_ga) + b_ga).reshape(B, T, D_RNN)
    i = jax.nn.sigmoid(jnp.einsum('btnc,ncd->btnd', xg, w_gx) + b_gx).reshape(B, T, D_RNN)
    log_a = -LRU_C * r.astype(jnp.float32) * jax.nn.softplus(-lam.astype(jnp.float32))
    a = jnp.exp(log_a)
    b = jnp.sqrt(-jnp.expm1(2.0 * log_a)) * (i * xb).astype(jnp.float32)

    def combine(lhs, rhs):
        a1, b1 = lhs
        a2, b2 = rhs
        return a1 * a2, a2 * b1 + b2

    _, hs = lax.associative_scan(combine, (a, b), axis=1)
    return (hs.astype(h.dtype) * yb) @ w_o


def swiglu(h, w_gu, w_down):
    g, u = jnp.split(h @ w_gu, 2, axis=-1)
    return (jax.nn.silu(g) * u) @ w_down


def setup_inputs(seed: int = 0) -> dict:
    key = jax.random.key(seed)
    ks = jax.random.split(key, 24)
    f32 = jnp.float32

    def dense(k, shape, fan_in):
        return jax.random.normal(k, shape, f32) * (fan_in ** -0.5)

    def gain(k, shape):
        return 1.0 + 0.02 * jax.random.normal(k, shape, f32)

    x = jax.random.normal(ks[0], (BATCH, SEQ, D_MODEL), f32)
    meta_tokens = jax.random.normal(ks[1], (N_META, D_MODEL), f32)
    norm_mix = gain(ks[2], (DEPTH, D_MODEL))
    norm_ffn = gain(ks[3], (DEPTH, D_MODEL))
    norm_final = gain(ks[4], (D_MODEL,))

    mla_w_in = dense(ks[5], (N_MLA, D_MODEL, Q_LORA + KV_LORA + QK_ROPE), D_MODEL)
    mla_q_norm = gain(ks[6], (N_MLA, Q_LORA))
    mla_kv_norm = gain(ks[7], (N_MLA, KV_LORA))
    mla_w_uq = dense(ks[8], (N_MLA, Q_LORA, MLA_HEADS * (QK_NOPE + QK_ROPE)), Q_LORA)
    mla_w_ukv = dense(ks[9], (N_MLA, KV_LORA, MLA_HEADS * (QK_NOPE + V_HEAD)), KV_LORA)
    mla_w_o = dense(ks[10], (N_MLA, MLA_HEADS * V_HEAD, D_MODEL), MLA_HEADS * V_HEAD)

    lru_w_in = dense(ks[11], (N_LRU, D_MODEL, 2 * D_RNN), D_MODEL)
    lru_conv_w = dense(ks[12], (N_LRU, CONV_W, D_RNN), CONV_W)
    lru_conv_b = 0.01 * jax.random.normal(ks[13], (N_LRU, D_RNN), f32)
    lru_w_gate_a = dense(ks[14], (N_LRU, RNN_BLOCKS, RNN_BW, RNN_BW), RNN_BW)
    lru_b_gate_a = 0.01 * jax.random.normal(ks[15], (N_LRU, RNN_BLOCKS, RNN_BW), f32)
    lru_w_gate_x = dense(ks[16], (N_LRU, RNN_BLOCKS, RNN_BW, RNN_BW), RNN_BW)
    lru_b_gate_x = 0.01 * jax.random.normal(ks[17], (N_LRU, RNN_BLOCKS, RNN_BW), f32)
    a_c = jax.random.uniform(ks[18], (N_LRU, D_RNN), f32, 0.9, 0.999)
    a0 = a_c ** (1.0 / LRU_C)
    lru_lambda = jnp.log(a0) - jnp.log1p(-a0)
    lru_w_o = dense(ks[19], (N_LRU, D_RNN, D_MODEL), D_RNN)

    ffn_w_gu = dense(ks[20], (DEPTH, D_MODEL, 2 * D_FF), D_MODEL)
    ffn_w_down = dense(ks[21], (DEPTH, D_FF, D_MODEL), D_FF)

    return {
        'x': x, 'meta_tokens': meta_tokens,
        'norm_mix': norm_mix, 'norm_ffn': norm_ffn, 'norm_final': norm_final,
        'mla_w_in': mla_w_in, 'mla_q_norm': mla_q_norm, 'mla_kv_norm': mla_kv_norm,
        'mla_w_uq': mla_w_uq, 'mla_w_ukv': mla_w_ukv, 'mla_w_o': mla_w_o,
        'lru_w_in': lru_w_in, 'lru_conv_w': lru_conv_w, 'lru_conv_b': lru_conv_b,
        'lru_w_gate_a': lru_w_gate_a, 'lru_b_gate_a': lru_b_gate_a,
        'lru_w_gate_x': lru_w_gate_x, 'lru_b_gate_x': lru_b_gate_x,
        'lru_lambda': lru_lambda, 'lru_w_o': lru_w_o,
        'ffn_w_gu': ffn_w_gu, 'ffn_w_down': ffn_w_down,
    }


def reference(x, meta_tokens, norm_mix, norm_ffn, norm_final,
              mla_w_in, mla_q_norm, mla_kv_norm, mla_w_uq, mla_w_ukv, mla_w_o,
              lru_w_in, lru_conv_w, lru_conv_b, lru_w_gate_a, lru_b_gate_a,
              lru_w_gate_x, lru_b_gate_x, lru_lambda, lru_w_o,
              ffn_w_gu, ffn_w_down):
    B = x.shape[0]
    meta = jnp.broadcast_to(meta_tokens.astype(x.dtype)[None], (B, N_META, D_MODEL))
    h = jnp.concatenate([meta, x], axis=1)
    for layer in range(DEPTH):
        j = layer // N_MIXERS
        hn = rms_norm(h, norm_mix[layer])
        if layer % N_MIXERS == 0:
            mix = mla_mixer(hn, mla_w_in[j], mla_q_norm[j], mla_kv_norm[j],
                            mla_w_uq[j], mla_w_ukv[j], mla_w_o[j])
        else:
            mix = rglru_mixer(hn, lru_w_in[j], lru_conv_w[j], lru_conv_b[j],
                              lru_w_gate_a[j], lru_b_gate_a[j],
                              lru_w_gate_x[j], lru_b_gate_x[j],
                              lru_lambda[j], lru_w_o[j])
        h = h + mix
        h = h + swiglu(rms_norm(h, norm_ffn[layer]), ffn_w_gu[layer], ffn_w_down[layer])
    h = rms_norm(h, norm_final)
    return h[:, N_META:]
```

```python
import functools

import jax
import jax.numpy as jnp
from jax import lax
from jax.experimental import pallas as pl
from jax.experimental.pallas import tpu as pltpu

D_MODEL = 2048
SEQ = 4096
CHUNK = 64
N_META = 16
DEPTH = 4
N_MIXERS = 2

MLA_HEADS = 16
Q_LORA = 512
KV_LORA = 512
QK_NOPE = 128
QK_ROPE = 64
V_HEAD = 128
ROPE_THETA = 10000.0

D_RNN = D_MODEL
RNN_BLOCKS = 16
RNN_BW = D_RNN // RNN_BLOCKS
CONV_W = 4
LRU_C = 8.0

D_FF = -(-8 * D_MODEL // (3 * 256)) * 256

RMS_EPS = 1e-6
NEG_BIG = -1e30

LANE = 128
MXU_DIM = 256
PAD = LANE - N_META
TP = PAD + N_META + SEQ
VMEM_LIMIT = 56 * 1024 * 1024

ROW_TILE = 512
FF_TILE = 512
Q_TILE = 384
KV_TILE = 256
T_TILE = 384
GATE_TILE = 2 * RNN_BW

BF16 = jnp.bfloat16
F32 = jnp.float32


def _params(*sem):
    return pltpu.CompilerParams(dimension_semantics=sem, vmem_limit_bytes=VMEM_LIMIT)


def _rms(x, g):
    ms = jnp.mean(x * x, axis=-1, keepdims=True)
    return x * lax.rsqrt(ms + RMS_EPS) * g


def _dot(a, b):
    return jnp.dot(a, b, preferred_element_type=F32)


def _const_spec(shape):
    return pl.BlockSpec(shape, lambda *_: (0,) * len(shape))


def _mla_proj_kernel(h_ref, g_ref, win_ref, qg_ref, kvg_ref, wuq_ref, wukv_ref,
                     cos_ref, sin_ref, qn_ref, qr_ref, kn_ref, kr_ref, v_ref):
    scale = (QK_NOPE + QK_ROPE) ** -0.5
    hn = _rms(h_ref[...], g_ref[...]).astype(BF16)
    proj = _dot(hn, win_ref[...])
    cq = _rms(proj[:, :Q_LORA], qg_ref[...]).astype(BF16)
    ckv = _rms(proj[:, Q_LORA:Q_LORA + KV_LORA], kvg_ref[...]).astype(BF16)
    cos = cos_ref[...]
    sin = sin_ref[...]
    o = Q_LORA + KV_LORA
    for p in range(2):
        raw = proj[:, o + p * LANE:o + (p + 1) * LANE]
        rot = proj[:, o + (2 + p) * LANE:o + (3 + p) * LANE]
        kr_ref[:, p * LANE:(p + 1) * LANE] = (raw * cos + rot * sin).astype(BF16)
    q = _dot(cq, wuq_ref[...])
    hn_w = MLA_HEADS * QK_NOPE
    hr_w = MLA_HEADS * QK_ROPE
    qn_ref[...] = (q[:, :hn_w] * scale).astype(BF16)
    for p in range(hr_w // LANE):
        raw = q[:, hn_w + p * LANE:hn_w + (p + 1) * LANE]
        rot = q[:, hn_w + hr_w + p * LANE:hn_w + hr_w + (p + 1) * LANE]
        qr_ref[:, p * LANE:(p + 1) * LANE] = ((raw * cos + rot * sin) * scale).astype(BF16)
    kv = _dot(ckv, wukv_ref[...])
    kn_ref[...] = kv[:, :hn_w].astype(BF16)
    v_ref[...] = kv[:, hn_w:].astype(BF16)


def _mla_proj(h, g, win, qg, kvg, wuq, wukv, cos, sin):
    rows = h.shape[0]
    tm = ROW_TILE // 2
    hn_w = MLA_HEADS * QK_NOPE
    hr_w = MLA_HEADS * QK_ROPE
    row = lambda w: pl.BlockSpec((tm, w), lambda i: (i, 0))
    out_w = (hn_w, hr_w, hn_w, 2 * LANE, hn_w)
    return pl.pallas_call(
        _mla_proj_kernel,
        grid=(rows // tm,),
        in_specs=[row(D_MODEL), _const_spec(g.shape), _const_spec(win.shape),
                  _const_spec(qg.shape), _const_spec(kvg.shape), _const_spec(wuq.shape),
                  _const_spec(wukv.shape), row(LANE), row(LANE)],
        out_specs=[row(w) for w in out_w],
        out_shape=[jax.ShapeDtypeStruct((rows, w), BF16) for w in out_w],
        compiler_params=_params("parallel"),
        name="mla_proj",
    )(h, g, win, qg, kvg, wuq, wukv, cos, sin)


def _attn_kernel(qn_ref, qr_ref, kn_ref, kr_ref, v_ref, o_ref, m_sc, l_sc, acc_sc):
    i = pl.program_id(2)
    q = jnp.concatenate([qn_ref[0], qr_ref[0]], axis=1)
    m_sc[...] = jnp.full(m_sc.shape, -3.0e38, F32)
    l_sc[...] = jnp.zeros(l_sc.shape, F32)
    acc_sc[...] = jnp.zeros(acc_sc.shape, F32)
    qrow = i * Q_TILE + lax.broadcasted_iota(jnp.int32, (Q_TILE, 1), 0)
    vis_end = jnp.maximum(PAD + N_META, (qrow // CHUNK + 1) * CHUNK)
    k_end = (i + 1) * Q_TILE

    def step(start, size):
        k = jnp.concatenate([kn_ref[0, pl.ds(start, size), :],
                             kr_ref[0, pl.ds(start, size), :]], axis=1)
        s = lax.dot_general(q, k, (((1,), (1,)), ((), ())), preferred_element_type=F32)
        kcol = start + lax.broadcasted_iota(jnp.int32, (1, size), 1)
        s = jnp.where((kcol >= PAD) & (kcol < vis_end), s, NEG_BIG)
        m_prev = m_sc[...]
        m_new = jnp.maximum(m_prev, jnp.max(s, axis=-1, keepdims=True))
        alpha = jnp.exp(m_prev - m_new)
        p = jnp.exp(s - m_new)
        l_sc[...] = alpha * l_sc[...] + jnp.sum(p, axis=-1, keepdims=True)
        acc_sc[...] = alpha * acc_sc[...] + _dot(p.astype(BF16), v_ref[0, pl.ds(start, size), :])
        m_sc[...] = m_new

    n_full = k_end // KV_TILE

    def body(j, carry):
        step(pl.multiple_of(j * KV_TILE, KV_TILE), KV_TILE)
        return carry

    lax.fori_loop(0, n_full, body, 0)

    rem = Q_TILE % KV_TILE

    @pl.when(k_end % KV_TILE != 0)
    def _():
        step(pl.multiple_of(n_full * KV_TILE, rem), rem)

    o_ref[0] = (acc_sc[...] / l_sc[...]).astype(BF16)


def _attention(qn, qr, kn, kr, v):
    batch = qn.shape[0]
    q_spec = lambda f: pl.BlockSpec((1, Q_TILE, LANE), f)
    kv_spec = lambda f: pl.BlockSpec((1, TP, LANE), f)
    return pl.pallas_call(
        _attn_kernel,
        grid=(batch, MLA_HEADS, TP // Q_TILE),
        in_specs=[q_spec(lambda b, h, i: (b, i, h)),
                  q_spec(lambda b, h, i: (b, i, h // 2)),
                  kv_spec(lambda b, h, i: (b, 0, h)),
                  kv_spec(lambda b, h, i: (b, 0, h % 2)),
                  kv_spec(lambda b, h, i: (b, 0, h))],
        out_specs=q_spec(lambda b, h, i: (b, i, h)),
        out_shape=jax.ShapeDtypeStruct((batch, TP, MLA_HEADS * V_HEAD), BF16),
        scratch_shapes=[pltpu.VMEM((Q_TILE, 1), F32), pltpu.VMEM((Q_TILE, 1), F32),
                        pltpu.VMEM((Q_TILE, V_HEAD), F32)],
        compiler_params=_params("parallel", "parallel", "arbitrary"),
        name="mla_attention",
    )(qn, qr, kn, kr, v)


def _proj_residual_kernel(h_ref, a_ref, w_ref, o_ref):
    o_ref[...] = h_ref[...] + _dot(a_ref[...], w_ref[...])


def _proj_residual(h, a, w):
    rows = h.shape[0]
    tm = ROW_TILE
    row = lambda width: pl.BlockSpec((tm, width), lambda i: (i, 0))
    return pl.pallas_call(
        _proj_residual_kernel,
        grid=(rows // tm,),
        in_specs=[row(D_MODEL), row(a.shape[1]), _const_spec(w.shape)],
        out_specs=row(D_MODEL),
        out_shape=jax.ShapeDtypeStruct(h.shape, F32),
        compiler_params=_params("parallel"),
        name="proj_residual",
    )(h, a, w)


def _ffn_kernel(h_ref, g_ref, wg_ref, wu_ref, wd_ref, o_ref, hn_sc):
    @pl.when(pl.program_id(1) == 0)
    def _():
        x = h_ref[...]
        hn_sc[...] = _rms(x, g_ref[...]).astype(BF16)
        o_ref[...] = x

    hn = hn_sc[...]
    gate = _dot(hn, wg_ref[...])
    up = _dot(hn, wu_ref[...])
    act = (gate * jax.nn.sigmoid(gate) * up).astype(BF16)
    o_ref[...] += _dot(act, wd_ref[...])


def _ffn(h, g, w_gu, w_down):
    rows = h.shape[0]
    tm = ROW_TILE
    n_f = D_FF // FF_TILE
    return pl.pallas_call(
        _ffn_kernel,
        grid=(rows // tm, n_f),
        in_specs=[pl.BlockSpec((tm, D_MODEL), lambda i, j: (i, 0)),
                  pl.BlockSpec((1, D_MODEL), lambda i, j: (0, 0)),
                  pl.BlockSpec((D_MODEL, FF_TILE), lambda i, j: (0, j)),
                  pl.BlockSpec((D_MODEL, FF_TILE), lambda i, j: (0, j + n_f)),
                  pl.BlockSpec((FF_TILE, D_MODEL), lambda i, j: (j, 0))],
        out_specs=pl.BlockSpec((tm, D_MODEL), lambda i, j: (i, 0)),
        out_shape=jax.ShapeDtypeStruct(h.shape, F32),
        scratch_shapes=[pltpu.VMEM((tm, D_MODEL), BF16)],
        compiler_params=_params("parallel", "arbitrary"),
        name="ffn",
    )(h, g, w_gu, w_gu, w_down)


def _norm_proj_kernel(h_ref, g_ref, w_ref, o_ref, *, gelu):
    hn = _rms(h_ref[...], g_ref[...]).astype(BF16)
    y = _dot(hn, w_ref[...])
    if gelu:
        c = (2.0 / jnp.pi) ** 0.5
        y = 0.5 * y * (1.0 + jnp.tanh(c * (y + 0.044715 * (y * y * y))))
    o_ref[...] = y


def _norm_proj(h, g, w, gelu):
    rows = h.shape[0]
    tm = ROW_TILE
    row = lambda width: pl.BlockSpec((tm, width), lambda i: (i, 0))
    return pl.pallas_call(
        functools.partial(_norm_proj_kernel, gelu=gelu),
        grid=(rows // tm,),
        in_specs=[row(D_MODEL), _const_spec(g.shape), _const_spec(w.shape)],
        out_specs=row(w.shape[1]),
        out_shape=jax.ShapeDtypeStruct((rows, w.shape[1]), F32),
        compiler_params=_params("parallel"),
        name="lru_in_gelu" if gelu else "lru_in",
    )(h, g, w)


def _lru_kernel(xb_ref, yg_ref, cw_ref, cb_ref, wga_ref, bga_ref, wgx_ref, bgx_ref, lam_ref,
                o_ref, xpad_sc, a_sc, b_sc, h_sc):
    t = pl.program_id(1)
    halo = 8

    @pl.when(t == 0)
    def _():
        xpad_sc[0:halo, :] = jnp.zeros((halo, D_RNN), F32)
        h_sc[...] = jnp.zeros(h_sc.shape, F32)

    row = t * T_TILE + lax.broadcasted_iota(jnp.int32, (T_TILE, 1), 0)
    valid = row >= PAD
    xpad_sc[halo:halo + T_TILE, :] = jnp.where(valid, xb_ref[0], 0.0)

    lam = lam_ref[...]
    neg_lam = -lam
    softplus = jnp.maximum(neg_lam, 0.0) + jnp.log1p(jnp.exp(-jnp.abs(neg_lam)))
    for g in range(D_RNN // GATE_TILE):
        cols = slice(g * GATE_TILE, (g + 1) * GATE_TILE)
        xc = cb_ref[:, cols]
        for k in range(CONV_W):
            o = halo - (CONV_W - 1) + k
            xc = xc + cw_ref[k:k + 1, cols] * xpad_sc[o:o + T_TILE, cols]
        xc16 = xc.astype(BF16)
        r = jax.nn.sigmoid(_dot(xc16, wga_ref[g]) + bga_ref[:, cols])
        ig = jax.nn.sigmoid(_dot(xc16, wgx_ref[g]) + bgx_ref[:, cols])
        log_a = -LRU_C * r * softplus[:, cols]
        a = jnp.exp(log_a)
        a_sc[:, cols] = a
        bb = jnp.sqrt(jnp.tanh(-log_a) * (1.0 + a * a)) * (ig * xc)
        b_sc[:, cols] = jnp.where(valid, bb, 0.0)

    xpad_sc[0:halo, :] = xpad_sc[T_TILE:T_TILE + halo, :]

    def body(s, h):
        h = a_sc[pl.ds(s, 1), :] * h + b_sc[pl.ds(s, 1), :]
        b_sc[pl.ds(s, 1), :] = h
        return h

    h_sc[...] = lax.fori_loop(0, T_TILE, body, h_sc[...], unroll=8)
    o_ref[0] = (b_sc[...] * yg_ref[0]).astype(BF16)


def _lru(xb, yg, cw, cb, wga, bga, wgx, bgx, lam):
    batch = xb.shape[0]
    blk = pl.BlockSpec((1, T_TILE, D_RNN), lambda b, t: (b, t, 0))
    consts = (cw, cb, wga, bga, wgx, bgx, lam)
    return pl.pallas_call(
        _lru_kernel,
        grid=(batch, TP // T_TILE),
        in_specs=[blk, blk] + [_const_spec(c.shape) for c in consts],
        out_specs=blk,
        out_shape=jax.ShapeDtypeStruct(xb.shape, BF16),
        scratch_shapes=[pltpu.VMEM((T_TILE + 8, D_RNN), F32), pltpu.VMEM((T_TILE, D_RNN), F32),
                        pltpu.VMEM((T_TILE, D_RNN), F32), pltpu.VMEM((1, D_RNN), F32)],
        compiler_params=_params("parallel", "arbitrary"),
        name="rglru",
    )(xb, yg, *consts)


def _final_norm_kernel(h_ref, g_ref, o_ref):
    o_ref[0] = _rms(h_ref[0], g_ref[...])


def _final_norm(h, g):
    batch = h.shape[0]
    return pl.pallas_call(
        _final_norm_kernel,
        grid=(batch, SEQ // LANE),
        in_specs=[pl.BlockSpec((1, LANE, D_MODEL), lambda b, i: (b, i + 1, 0)),
                  _const_spec(g.shape)],
        out_specs=pl.BlockSpec((1, LANE, D_MODEL), lambda b, i: (b, i, 0)),
        out_shape=jax.ShapeDtypeStruct((batch, SEQ, D_MODEL), F32),
        compiler_params=_params("parallel", "parallel"),
        name="final_norm",
    )(h, g)


def _rot_half_cols(w):
    return jnp.roll(w, -QK_ROPE // 2, axis=-1)


def _mla_weights(w_in, w_uq, w_ukv):
    w_q = w_in[:, :Q_LORA]
    w_kv = w_in[:, Q_LORA:Q_LORA + KV_LORA]
    w_kr = w_in[:, Q_LORA + KV_LORA:]
    z = jnp.zeros_like(w_kr)
    pair = lambda w: [w, z, z, w]
    win = jnp.concatenate([w_q, w_kv] + pair(w_kr) + pair(_rot_half_cols(w_kr)), axis=1)
    uq = w_uq.reshape(Q_LORA, MLA_HEADS, QK_NOPE + QK_ROPE)
    uq_n = uq[:, :, :QK_NOPE].reshape(Q_LORA, -1)
    uq_r = uq[:, :, QK_NOPE:]
    wuq = jnp.concatenate([uq_n, uq_r.reshape(Q_LORA, -1),
                           _rot_half_cols(uq_r).reshape(Q_LORA, -1)], axis=1)
    ukv = w_ukv.reshape(KV_LORA, MLA_HEADS, QK_NOPE + V_HEAD)
    wukv = jnp.concatenate([ukv[:, :, :QK_NOPE].reshape(KV_LORA, -1),
                            ukv[:, :, QK_NOPE:].reshape(KV_LORA, -1)], axis=1)
    return win.astype(BF16), wuq.astype(BF16), wukv.astype(BF16)


def _gate_block_diag(w):
    w = w.reshape(RNN_BLOCKS // 2, 2, RNN_BW, RNN_BW)
    z = jnp.zeros_like(w[:, 0])
    top = jnp.concatenate([w[:, 0], z], axis=2)
    bot = jnp.concatenate([z, w[:, 1]], axis=2)
    return jnp.concatenate([top, bot], axis=1).astype(BF16)


def _rope_tables(batch):
    pos = jnp.maximum(jnp.arange(TP) - PAD, 0).astype(F32)
    inv_freq = ROPE_THETA ** (-jnp.arange(0, QK_ROPE, 2, dtype=F32) / QK_ROPE)
    ang = pos[:, None] * inv_freq[None, :]
    cos, sin = jnp.cos(ang), jnp.sin(ang)
    cos = jnp.concatenate([cos, cos, cos, cos], axis=1)
    sin = jnp.concatenate([-sin, sin, -sin, sin], axis=1)
    return jnp.tile(cos, (batch, 1)), jnp.tile(sin, (batch, 1))


def kernel(x, meta_tokens, norm_mix, norm_ffn, norm_final, mla_w_in, mla_q_norm, mla_kv_norm, mla_w_uq, mla_w_ukv, mla_w_o, lru_w_in, lru_conv_w, lru_conv_b, lru_w_gate_a, lru_b_gate_a, lru_w_gate_x, lru_b_gate_x, lru_lambda, lru_w_o, ffn_w_gu, ffn_w_down):
    batch = x.shape[0]
    rows = batch * TP
    meta = jnp.broadcast_to(meta_tokens.astype(x.dtype)[None], (batch, N_META, D_MODEL))
    h = jnp.concatenate([jnp.zeros((batch, PAD, D_MODEL), x.dtype), meta, x], axis=1)
    h = h.reshape(rows, D_MODEL)
    cos, sin = _rope_tables(batch)
    row_vec = lambda v: v.reshape(1, -1)
    seq = lambda a: a.reshape(batch, TP, a.shape[-1])

    for layer in range(DEPTH):
        j = layer // N_MIXERS
        g_mix = row_vec(norm_mix[layer])
        if layer % N_MIXERS == 0:
            win, wuq, wukv = _mla_weights(mla_w_in[j], mla_w_uq[j], mla_w_ukv[j])
            qn, qr, kn, kr, v = _mla_proj(h, g_mix, win, row_vec(mla_q_norm[j]),
                                          row_vec(mla_kv_norm[j]), wuq, wukv, cos, sin)
            mixed = _attention(seq(qn), seq(qr), seq(kn), seq(kr), seq(v)).reshape(rows, -1)
            w_o = mla_w_o[j].astype(BF16)
        else:
            w_in = lru_w_in[j].astype(BF16)
            xb = _norm_proj(h, g_mix, w_in[:, :D_RNN], gelu=False)
            yg = _norm_proj(h, g_mix, w_in[:, D_RNN:], gelu=True)
            mixed = _lru(seq(xb), seq(yg), lru_conv_w[j], row_vec(lru_conv_b[j]),
                         _gate_block_diag(lru_w_gate_a[j]), row_vec(lru_b_gate_a[j]),
                         _gate_block_diag(lru_w_gate_x[j]), row_vec(lru_b_gate_x[j]),
                         row_vec(lru_lambda[j])).reshape(rows, -1)
            w_o = lru_w_o[j].astype(BF16)
        h = _proj_residual(h, mixed, w_o)
        h = _ffn(h, row_vec(norm_ffn[layer]), ffn_w_gu[layer].astype(BF16),
                 ffn_w_down[layer].astype(BF16))
    return _final_norm(h.reshape(batch, TP, D_MODEL), row_vec(norm_final))
```

```python
import functools
import math

import jax
import jax.numpy as jnp
from jax import lax
from jax.experimental import pallas as pl
from jax.experimental.pallas import tpu as pltpu

D_MODEL = 2048
SEQ = 4096
CHUNK = 64
N_META = 16
DEPTH = 4
N_MIXERS = 2

MLA_HEADS = 16
Q_LORA = 512
KV_LORA = 512
QK_NOPE = 128
QK_ROPE = 64
V_HEAD = 128
ROPE_THETA = 10000.0

D_RNN = D_MODEL
RNN_BLOCKS = 16
RNN_BW = D_RNN // RNN_BLOCKS
CONV_W = 4
LRU_C = 8.0

D_FF = -(-8 * D_MODEL // (3 * 256)) * 256

RMS_EPS = 1e-6
NEG_BIG = -1e30

LANE = 128
MXU_DIM = 256
PAD = LANE - N_META
TP = PAD + N_META + SEQ
VMEM_LIMIT = 56 * 1024 * 1024

ROW_TILE = 512
SEQ_TILE = 384
FF_TILE = 512
ATT_TILE = 512
TPA = -(-TP // ATT_TILE) * ATT_TILE
HEAD_GROUP = 4
GATE_TILE = 2 * RNN_BW

BF16 = jnp.bfloat16
F32 = jnp.float32


def _params(*sem):
    return pltpu.CompilerParams(dimension_semantics=sem, vmem_limit_bytes=VMEM_LIMIT)


def _rms(x, g):
    ms = jnp.mean(x * x, axis=-1, keepdims=True)
    return x * lax.rsqrt(ms + RMS_EPS) * g


def _dot(a, b):
    return jnp.dot(a, b, preferred_element_type=F32)


def _dot_nt(a, b):
    return lax.dot_general(a, b, (((1,), (1,)), ((), ())), preferred_element_type=F32)


def _const_spec(shape):
    return pl.BlockSpec(shape, lambda *_: (0,) * len(shape))


def _mla_proj_kernel(h_ref, g_ref, win_ref, qg_ref, kvg_ref, wuq_ref, wuk_ref, wvt_ref,
                     cos_ref, sin_ref, qn_ref, qr_ref, kn_ref, kr_ref, vt_ref):
    outs = (qn_ref, qr_ref, kn_ref, kr_ref, vt_ref)

    @pl.when(pl.program_id(1) == TP // SEQ_TILE)
    def _():
        for o_ref in outs:
            o_ref[...] = jnp.zeros(o_ref.shape, o_ref.dtype)

    @pl.when(pl.program_id(1) < TP // SEQ_TILE)
    def _():
        scale = (QK_NOPE + QK_ROPE) ** -0.5 * math.log2(math.e)
        hn = _rms(h_ref[0], g_ref[...]).astype(BF16)
        proj = _dot(hn, win_ref[...])
        cq = _rms(proj[:, :Q_LORA], qg_ref[...]).astype(BF16)
        ckv = _rms(proj[:, Q_LORA:Q_LORA + KV_LORA], kvg_ref[...]).astype(BF16)
        cos = cos_ref[...]
        sin = sin_ref[...]
        o = Q_LORA + KV_LORA
        for p in range(2):
            raw = proj[:, o + p * LANE:o + (p + 1) * LANE]
            rot = proj[:, o + (2 + p) * LANE:o + (3 + p) * LANE]
            kr_ref[0, :, p * LANE:(p + 1) * LANE] = (raw * cos + rot * sin).astype(BF16)
        q = _dot(cq, wuq_ref[...])
        hn_w = MLA_HEADS * QK_NOPE
        hr_w = MLA_HEADS * QK_ROPE
        qn_ref[0] = (q[:, :hn_w] * scale).astype(BF16)
        for p in range(hr_w // LANE):
            raw = q[:, hn_w + p * LANE:hn_w + (p + 1) * LANE]
            rot = q[:, hn_w + hr_w + p * LANE:hn_w + hr_w + (p + 1) * LANE]
            qr_ref[0, :, p * LANE:(p + 1) * LANE] = ((raw * cos + rot * sin) * scale).astype(BF16)
        kn_ref[0] = _dot(ckv, wuk_ref[...]).astype(BF16)
        vt_ref[...] = _dot_nt(wvt_ref[...], ckv).astype(BF16)


def _mla_proj(h, g, win, qg, kvg, wuq, wuk, wvt, cos, sin):
    batch = h.shape[0]
    tm = SEQ_TILE
    n_t = TP // tm
    n_ta = TPA // tm
    hn_w = MLA_HEADS * QK_NOPE
    hr_w = MLA_HEADS * QK_ROPE
    clamp = lambda t: jnp.minimum(t, n_t - 1)
    seq_out = lambda w: pl.BlockSpec((1, tm, w), lambda b, t: (b, t, 0))
    consts = (g, win, qg, kvg, wuq, wuk, wvt)
    return pl.pallas_call(
        _mla_proj_kernel,
        grid=(batch, n_ta),
        in_specs=[pl.BlockSpec((1, tm, D_MODEL), lambda b, t: (b, clamp(t), 0))]
        + [_const_spec(c.shape) for c in consts]
        + [pl.BlockSpec((tm, LANE), lambda b, t: (clamp(t), 0))] * 2,
        out_specs=[seq_out(hn_w), seq_out(hr_w), seq_out(hn_w), seq_out(2 * LANE),
                   pl.BlockSpec((hn_w, tm), lambda b, t: (0, b * n_ta + t))],
        out_shape=[jax.ShapeDtypeStruct((batch, TPA, hn_w), BF16),
                   jax.ShapeDtypeStruct((batch, TPA, hr_w), BF16),
                   jax.ShapeDtypeStruct((batch, TPA, hn_w), BF16),
                   jax.ShapeDtypeStruct((batch, TPA, 2 * LANE), BF16),
                   jax.ShapeDtypeStruct((hn_w, batch * TPA), BF16)],
        compiler_params=_params("parallel", "parallel"),
        name="mla_proj",
    )(h, *consts, cos, sin)


def _attn_kernel(qn_ref, qr_ref, kn_ref, kr_ref, vt_ref, o_ref, m_sc, l_sc, acc_sc):
    i = pl.program_id(2)
    t = ATT_TILE
    col_groups = t // MXU_DIM
    chains = []
    for hh in range(HEAD_GROUP):
        q = jnp.concatenate([qn_ref[0, :, hh * LANE:(hh + 1) * LANE],
                             qr_ref[0, :, (hh // 2) * LANE:(hh // 2 + 1) * LANE]], axis=1)
        for c in range(col_groups):
            chains.append((hh, c, q[c * MXU_DIM:(c + 1) * MXU_DIM, :]))

    m_sc[...] = jnp.full(m_sc.shape, -3.0e38, F32)
    l_sc[...] = jnp.zeros(l_sc.shape, F32)
    acc_sc[...] = jnp.zeros(acc_sc.shape, F32)

    def step(j, mask):
        start = pl.multiple_of(j * t, t)
        scores = []
        for hh, c, q in chains:
            k = jnp.concatenate([kn_ref[0, pl.ds(start, t), hh * LANE:(hh + 1) * LANE],
                                 kr_ref[0, pl.ds(start, t), (hh % 2) * LANE:(hh % 2 + 1) * LANE]], axis=1)
            scores.append(_dot_nt(k, q))
        for n, (hh, c, q) in enumerate(chains):
            s = scores[n]
            if mask is not None:
                s = jnp.where(mask(start, c), s, NEG_BIG)
            m_prev = m_sc[n]
            m_new = jnp.maximum(m_prev, jnp.max(s, axis=0, keepdims=True))
            alpha = jnp.exp2(m_prev - m_new)
            p = jnp.exp2(s - m_new)
            l_sc[n] = alpha * l_sc[n] + jnp.sum(p, axis=0, keepdims=True)
            v_t = vt_ref[hh * V_HEAD:(hh + 1) * V_HEAD, pl.ds(start, t)]
            acc_sc[n] = alpha * acc_sc[n] + _dot(v_t, p.astype(BF16))
            m_sc[n] = m_new

    def key_row(start):
        return start + lax.broadcasted_iota(jnp.int32, (t, 1), 0)

    def pad_mask(start, c):
        return key_row(start) >= PAD

    def diag_mask(start, c):
        qrow = i * t + c * MXU_DIM + lax.broadcasted_iota(jnp.int32, (1, MXU_DIM), 1)
        vis_end = jnp.maximum(PAD + N_META, (qrow // CHUNK + 1) * CHUNK)
        kk = key_row(start)
        return (kk >= PAD) & (kk < vis_end)

    @pl.when(i > 0)
    def _():
        step(0, pad_mask)

    def body(j, carry):
        step(j, None)
        return carry

    lax.fori_loop(1, i, body, 0)
    step(i, diag_mask)

    for n, (hh, c, _) in enumerate(chains):
        o_t = acc_sc[n] / l_sc[n]
        o_ref[0, c * MXU_DIM:(c + 1) * MXU_DIM, hh * V_HEAD:(hh + 1) * V_HEAD] = o_t.T.astype(BF16)


def _attention(qn, qr, kn, kr, vt):
    batch = qn.shape[0]
    t = ATT_TILE
    gw = HEAD_GROUP * LANE
    n_chain = HEAD_GROUP * (t // MXU_DIM)
    return pl.pallas_call(
        _attn_kernel,
        grid=(batch, MLA_HEADS // HEAD_GROUP, TPA // t),
        in_specs=[pl.BlockSpec((1, t, gw), lambda b, g, i: (b, i, g)),
                  pl.BlockSpec((1, t, gw // 2), lambda b, g, i: (b, i, g)),
                  pl.BlockSpec((1, TPA, gw), lambda b, g, i: (b, 0, g)),
                  pl.BlockSpec((1, TPA, 2 * LANE), lambda b, g, i: (b, 0, 0)),
                  pl.BlockSpec((gw, TPA), lambda b, g, i: (g, b))],
        out_specs=pl.BlockSpec((1, t, gw), lambda b, g, i: (b, i, g)),
        out_shape=jax.ShapeDtypeStruct((batch, TPA, MLA_HEADS * V_HEAD), BF16),
        scratch_shapes=[pltpu.VMEM((n_chain, 1, MXU_DIM), F32),
                        pltpu.VMEM((n_chain, 1, MXU_DIM), F32),
                        pltpu.VMEM((n_chain, V_HEAD, MXU_DIM), F32)],
        compiler_params=_params("parallel", "parallel", "arbitrary"),
        name="mla_attention",
    )(qn, qr, kn, kr, vt)


def _proj_residual_kernel(h_ref, a_ref, w_ref, o_ref):
    o_ref[0] = h_ref[0] + _dot(a_ref[0], w_ref[...])


def _proj_residual(h, a, w):
    batch = h.shape[0]
    tm = SEQ_TILE
    blk = lambda width: pl.BlockSpec((1, tm, width), lambda b, t: (b, t, 0))
    return pl.pallas_call(
        _proj_residual_kernel,
        grid=(batch, TP // tm),
        in_specs=[blk(D_MODEL), blk(a.shape[2]), _const_spec(w.shape)],
        out_specs=blk(D_MODEL),
        out_shape=jax.ShapeDtypeStruct(h.shape, F32),
        compiler_params=_params("parallel", "parallel"),
        name="proj_residual",
    )(h, a, w)


def _ffn_kernel(h_ref, g_ref, wg_ref, wu_ref, wd_ref, o_ref, hn_sc):
    @pl.when(pl.program_id(1) == 0)
    def _():
        x = h_ref[...]
        hn_sc[...] = _rms(x, g_ref[...]).astype(BF16)
        o_ref[...] = x

    hn = hn_sc[...]
    gate = _dot(hn, wg_ref[...])
    up = _dot(hn, wu_ref[...])
    act = (gate * jax.nn.sigmoid(gate) * up).astype(BF16)
    o_ref[...] += _dot(act, wd_ref[...])


def _ffn(h, g, w_gu, w_down):
    rows = h.shape[0]
    tm = ROW_TILE
    n_f = D_FF // FF_TILE
    return pl.pallas_call(
        _ffn_kernel,
        grid=(rows // tm, n_f),
        in_specs=[pl.BlockSpec((tm, D_MODEL), lambda i, j: (i, 0)),
                  pl.BlockSpec((1, D_MODEL), lambda i, j: (0, 0)),
                  pl.BlockSpec((D_MODEL, FF_TILE), lambda i, j: (0, j)),
                  pl.BlockSpec((D_MODEL, FF_TILE), lambda i, j: (0, j + n_f)),
                  pl.BlockSpec((FF_TILE, D_MODEL), lambda i, j: (j, 0))],
        out_specs=pl.BlockSpec((tm, D_MODEL), lambda i, j: (i, 0)),
        out_shape=jax.ShapeDtypeStruct(h.shape, F32),
        scratch_shapes=[pltpu.VMEM((tm, D_MODEL), BF16)],
        compiler_params=_params("parallel", "arbitrary"),
        name="ffn",
    )(h, g, w_gu, w_gu, w_down)


def _norm_proj_kernel(h_ref, g_ref, w_ref, o_ref, *, gelu):
    hn = _rms(h_ref[...], g_ref[...]).astype(BF16)
    y = _dot(hn, w_ref[...])
    if gelu:
        c = (2.0 / jnp.pi) ** 0.5
        y = 0.5 * y * (1.0 + jnp.tanh(c * (y + 0.044715 * (y * y * y))))
    o_ref[...] = y


def _norm_proj(h, g, w, gelu):
    rows = h.shape[0]
    tm = ROW_TILE
    row = lambda width: pl.BlockSpec((tm, width), lambda i: (i, 0))
    return pl.pallas_call(
        functools.partial(_norm_proj_kernel, gelu=gelu),
        grid=(rows // tm,),
        in_specs=[row(D_MODEL), _const_spec(g.shape), _const_spec(w.shape)],
        out_specs=row(w.shape[1]),
        out_shape=jax.ShapeDtypeStruct((rows, w.shape[1]), F32),
        compiler_params=_params("parallel"),
        name="lru_in_gelu" if gelu else "lru_in",
    )(h, g, w)


def _lru_kernel(xb_ref, yg_ref, cw_ref, cb_ref, wga_ref, bga_ref, wgx_ref, bgx_ref, lam_ref,
                o_ref, xpad_sc, a_sc, b_sc, h_sc):
    t = pl.program_id(1)
    tt = SEQ_TILE
    halo = 8

    @pl.when(t == 0)
    def _():
        xpad_sc[0:halo, :] = jnp.zeros((halo, D_RNN), F32)
        h_sc[...] = jnp.zeros(h_sc.shape, F32)

    row = t * tt + lax.broadcasted_iota(jnp.int32, (tt, 1), 0)
    valid = row >= PAD
    xpad_sc[halo:halo + tt, :] = jnp.where(valid, xb_ref[0], 0.0)

    lam = lam_ref[...]
    neg_lam = -lam
    softplus = jnp.maximum(neg_lam, 0.0) + jnp.log1p(jnp.exp(-jnp.abs(neg_lam)))
    for g in range(D_RNN // GATE_TILE):
        cols = slice(g * GATE_TILE, (g + 1) * GATE_TILE)
        xc = cb_ref[:, cols]
        for k in range(CONV_W):
            o = halo - (CONV_W - 1) + k
            xc = xc + cw_ref[k:k + 1, cols] * xpad_sc[o:o + tt, cols]
        xc16 = xc.astype(BF16)
        r = jax.nn.sigmoid(_dot(xc16, wga_ref[g]) + bga_ref[:, cols])
        ig = jax.nn.sigmoid(_dot(xc16, wgx_ref[g]) + bgx_ref[:, cols])
        log_a = -LRU_C * r * softplus[:, cols]
        a = jnp.exp(log_a)
        a_sc[:, cols] = a
        bb = jnp.sqrt(jnp.tanh(-log_a) * (1.0 + a * a)) * (ig * xc)
        b_sc[:, cols] = jnp.where(valid, bb, 0.0)

    xpad_sc[0:halo, :] = xpad_sc[tt:tt + halo, :]

    def body(s, h):
        h = a_sc[pl.ds(s, 1), :] * h + b_sc[pl.ds(s, 1), :]
        b_sc[pl.ds(s, 1), :] = h
        return h

    h_sc[...] = lax.fori_loop(0, tt, body, h_sc[...], unroll=8)
    o_ref[0] = (b_sc[...] * yg_ref[0]).astype(BF16)


def _lru(xb, yg, cw, cb, wga, bga, wgx, bgx, lam):
    batch = xb.shape[0]
    tt = SEQ_TILE
    blk = pl.BlockSpec((1, tt, D_RNN), lambda b, t: (b, t, 0))
    consts = (cw, cb, wga, bga, wgx, bgx, lam)
    return pl.pallas_call(
        _lru_kernel,
        grid=(batch, TP // tt),
        in_specs=[blk, blk] + [_const_spec(c.shape) for c in consts],
        out_specs=blk,
        out_shape=jax.ShapeDtypeStruct(xb.shape, BF16),
        scratch_shapes=[pltpu.VMEM((tt + 8, D_RNN), F32), pltpu.VMEM((tt, D_RNN), F32),
                        pltpu.VMEM((tt, D_RNN), F32), pltpu.VMEM((1, D_RNN), F32)],
        compiler_params=_params("parallel", "arbitrary"),
        name="rglru",
    )(xb, yg, *consts)


def _final_norm_kernel(h_ref, g_ref, o_ref):
    o_ref[0] = _rms(h_ref[0], g_ref[...])


def _final_norm(h, g):
    batch = h.shape[0]
    return pl.pallas_call(
        _final_norm_kernel,
        grid=(batch, SEQ // LANE),
        in_specs=[pl.BlockSpec((1, LANE, D_MODEL), lambda b, i: (b, i + 1, 0)),
                  _const_spec(g.shape)],
        out_specs=pl.BlockSpec((1, LANE, D_MODEL), lambda b, i: (b, i, 0)),
        out_shape=jax.ShapeDtypeStruct((batch, SEQ, D_MODEL), F32),
        compiler_params=_params("parallel", "parallel"),
        name="final_norm",
    )(h, g)


def _rot_half_cols(w):
    return jnp.roll(w, -QK_ROPE // 2, axis=-1)


def _mla_weights(w_in, w_uq, w_ukv):
    w_q = w_in[:, :Q_LORA]
    w_kv = w_in[:, Q_LORA:Q_LORA + KV_LORA]
    w_kr = w_in[:, Q_LORA + KV_LORA:]
    z = jnp.zeros_like(w_kr)
    pair = lambda w: [w, z, z, w]
    win = jnp.concatenate([w_q, w_kv] + pair(w_kr) + pair(_rot_half_cols(w_kr)), axis=1)
    uq = w_uq.reshape(Q_LORA, MLA_HEADS, QK_NOPE + QK_ROPE)
    uq_n = uq[:, :, :QK_NOPE].reshape(Q_LORA, -1)
    uq_r = uq[:, :, QK_NOPE:]
    wuq = jnp.concatenate([uq_n, uq_r.reshape(Q_LORA, -1),
                           _rot_half_cols(uq_r).reshape(Q_LORA, -1)], axis=1)
    ukv = w_ukv.reshape(KV_LORA, MLA_HEADS, QK_NOPE + V_HEAD)
    wuk = ukv[:, :, :QK_NOPE].reshape(KV_LORA, -1)
    wvt = ukv[:, :, QK_NOPE:].reshape(KV_LORA, -1).T
    return win.astype(BF16), wuq.astype(BF16), wuk.astype(BF16), wvt.astype(BF16)


def _gate_block_diag(w):
    w = w.reshape(RNN_BLOCKS // 2, 2, RNN_BW, RNN_BW)
    z = jnp.zeros_like(w[:, 0])
    top = jnp.concatenate([w[:, 0], z], axis=2)
    bot = jnp.concatenate([z, w[:, 1]], axis=2)
    return jnp.concatenate([top, bot], axis=1).astype(BF16)


def _rope_tables():
    pos = jnp.maximum(jnp.arange(TP) - PAD, 0).astype(F32)
    inv_freq = ROPE_THETA ** (-jnp.arange(0, QK_ROPE, 2, dtype=F32) / QK_ROPE)
    ang = pos[:, None] * inv_freq[None, :]
    cos, sin = jnp.cos(ang), jnp.sin(ang)
    cos = jnp.concatenate([cos, cos, cos, cos], axis=1)
    sin = jnp.concatenate([-sin, sin, -sin, sin], axis=1)
    return cos, sin


def _mla_layer(h, g_mix, w_in, qg, kvg, w_uq, w_ukv, w_o, cos, sin):
    win, wuq, wuk, wvt = _mla_weights(w_in, w_uq, w_ukv)
    qn, qr, kn, kr, vt = _mla_proj(h, g_mix, win, qg, kvg, wuq, wuk, wvt, cos, sin)
    return _proj_residual(h, _attention(qn, qr, kn, kr, vt), w_o.astype(BF16))


def _lru_layer(h, g_mix, w_in, cw, cb, w_ga, b_ga, w_gx, b_gx, lam, w_o):
    batch = h.shape[0]
    hf = h.reshape(batch * TP, D_MODEL)
    w_in = w_in.astype(BF16)
    xb = _norm_proj(hf, g_mix, w_in[:, :D_RNN], gelu=False).reshape(batch, TP, D_RNN)
    yg = _norm_proj(hf, g_mix, w_in[:, D_RNN:], gelu=True).reshape(batch, TP, D_RNN)
    mixed = _lru(xb, yg, cw, cb, _gate_block_diag(w_ga), b_ga, _gate_block_diag(w_gx), b_gx, lam)
    return _proj_residual(h, mixed, w_o.astype(BF16))


def kernel(x, meta_tokens, norm_mix, norm_ffn, norm_final, mla_w_in, mla_q_norm, mla_kv_norm, mla_w_uq, mla_w_ukv, mla_w_o, lru_w_in, lru_conv_w, lru_conv_b, lru_w_gate_a, lru_b_gate_a, lru_w_gate_x, lru_b_gate_x, lru_lambda, lru_w_o, ffn_w_gu, ffn_w_down):
    batch = x.shape[0]
    rows = batch * TP
    meta = jnp.broadcast_to(meta_tokens.astype(x.dtype)[None], (batch, N_META, D_MODEL))
    h = jnp.concatenate([jnp.zeros((batch, PAD, D_MODEL), x.dtype), meta, x], axis=1)
    cos, sin = _rope_tables()
    row_vec = lambda v: v.reshape(1, -1)
    flat = lambda a: a.reshape(rows, a.shape[-1])
    seq = lambda a: a.reshape(batch, TP, a.shape[-1])

    for layer in range(DEPTH):
        j = layer // N_MIXERS
        g_mix = row_vec(norm_mix[layer])
        if layer % N_MIXERS == 0:
            h = _mla_layer(h, g_mix, mla_w_in[j], row_vec(mla_q_norm[j]), row_vec(mla_kv_norm[j]),
                           mla_w_uq[j], mla_w_ukv[j], mla_w_o[j], cos, sin)
        else:
            h = _lru_layer(h, g_mix, lru_w_in[j], lru_conv_w[j], row_vec(lru_conv_b[j]),
                           lru_w_gate_a[j], row_vec(lru_b_gate_a[j]), lru_w_gate_x[j],
                           row_vec(lru_b_gate_x[j]), row_vec(lru_lambda[j]), lru_w_o[j])
        h = seq(_ffn(flat(h), row_vec(norm_ffn[layer]), ffn_w_gu[layer].astype(BF16),
                     ffn_w_down[layer].astype(BF16)))
    return _final_norm(h, row_vec(norm_final))
```

```python
import math

import jax
import jax.numpy as jnp
from jax import lax
from jax.experimental import pallas as pl
from jax.experimental.pallas import tpu as pltpu

D_MODEL = 2048
SEQ = 4096
CHUNK = 64
N_META = 16
DEPTH = 4
N_MIXERS = 2

MLA_HEADS = 16
Q_LORA = 512
KV_LORA = 512
QK_NOPE = 128
QK_ROPE = 64
V_HEAD = 128
ROPE_THETA = 10000.0

D_RNN = D_MODEL
RNN_BLOCKS = 16
RNN_BW = D_RNN // RNN_BLOCKS
CONV_W = 4
LRU_C = 8.0

D_FF = -(-8 * D_MODEL // (3 * 256)) * 256

RMS_EPS = 1e-6
NEG_BIG = -1e30

LANE = 128
MXU_DIM = 256
PAD = LANE - N_META
TP = PAD + N_META + SEQ
VMEM_LIMIT = 56 * 1024 * 1024

ROW_TILE = 512
SEQ_TILE = 384
FFN_ROW_TILE = 768
FF_TILE = 512
ATT_TILE = 512
HEAD_GROUP = 4
GATE_TILE = 2 * RNN_BW

BF16 = jnp.bfloat16
F32 = jnp.float32


def _params(*sem):
    return pltpu.CompilerParams(dimension_semantics=sem, vmem_limit_bytes=VMEM_LIMIT)


def _rms(x, g):
    ms = jnp.mean(x * x, axis=-1, keepdims=True)
    return x * lax.rsqrt(ms + RMS_EPS) * g


def _dot(a, b):
    return jnp.dot(a, b, preferred_element_type=F32)


def _dot_nt(a, b):
    return lax.dot_general(a, b, (((1,), (1,)), ((), ())), preferred_element_type=F32)


def _const_spec(shape):
    return pl.BlockSpec(shape, lambda *_: (0,) * len(shape))


def _mla_proj_kernel(h_ref, g_ref, win_ref, qg_ref, kvg_ref, wuq_ref, wuk_ref, wvt_ref,
                     cos_ref, sin_ref, qn_ref, qr_ref, kn_ref, kr_ref, vt_ref):
    scale = (QK_NOPE + QK_ROPE) ** -0.5 * math.log2(math.e)
    hn = _rms(h_ref[0], g_ref[...]).astype(BF16)
    proj = _dot(hn, win_ref[...])
    cq = _rms(proj[:, :Q_LORA], qg_ref[...]).astype(BF16)
    ckv = _rms(proj[:, Q_LORA:Q_LORA + KV_LORA], kvg_ref[...]).astype(BF16)
    cos = cos_ref[...]
    sin = sin_ref[...]
    o = Q_LORA + KV_LORA
    for p in range(2):
        raw = proj[:, o + p * LANE:o + (p + 1) * LANE]
        rot = proj[:, o + (2 + p) * LANE:o + (3 + p) * LANE]
        kr_ref[0, :, p * LANE:(p + 1) * LANE] = (raw * cos + rot * sin).astype(BF16)
    q = _dot(cq, wuq_ref[...])
    hn_w = MLA_HEADS * QK_NOPE
    hr_w = MLA_HEADS * QK_ROPE
    qn_ref[0] = (q[:, :hn_w] * scale).astype(BF16)
    for p in range(hr_w // LANE):
        raw = q[:, hn_w + p * LANE:hn_w + (p + 1) * LANE]
        rot = q[:, hn_w + hr_w + p * LANE:hn_w + hr_w + (p + 1) * LANE]
        qr_ref[0, :, p * LANE:(p + 1) * LANE] = ((raw * cos + rot * sin) * scale).astype(BF16)
    kn_ref[0] = _dot(ckv, wuk_ref[...]).astype(BF16)
    vt_ref[...] = _dot_nt(wvt_ref[...], ckv).astype(BF16)


def _mla_proj(h, g, win, qg, kvg, wuq, wuk, wvt, cos, sin):
    batch = h.shape[0]
    tm = SEQ_TILE
    n_t = TP // tm
    hn_w = MLA_HEADS * QK_NOPE
    hr_w = MLA_HEADS * QK_ROPE
    seq_blk = lambda w: pl.BlockSpec((1, tm, w), lambda b, t: (b, t, 0))
    consts = (g, win, qg, kvg, wuq, wuk, wvt)
    out_w = (hn_w, hr_w, hn_w, 2 * LANE)
    return pl.pallas_call(
        _mla_proj_kernel,
        grid=(batch, n_t),
        in_specs=[seq_blk(D_MODEL)] + [_const_spec(c.shape) for c in consts]
        + [pl.BlockSpec((tm, LANE), lambda b, t: (t, 0))] * 2,
        out_specs=[seq_blk(w) for w in out_w]
        + [pl.BlockSpec((hn_w, tm), lambda b, t: (0, b * n_t + t))],
        out_shape=[jax.ShapeDtypeStruct((batch, TP, w), BF16) for w in out_w]
        + [jax.ShapeDtypeStruct((hn_w, batch * TP), BF16)],
        compiler_params=_params("parallel", "parallel"),
        name="mla_proj",
    )(h, *consts, cos, sin)


def _attn_kernel(qn_ref, qr_ref, qnm_ref, qrm_ref, kn_ref, kr_ref, vt_ref, o_ref,
                 s_sc, m_sc, l_sc, acc_sc):
    i = pl.program_id(2)
    t = ATT_TILE
    base = PAD + N_META
    heads = range(HEAD_GROUP)
    h_cols = lambda hh: slice(hh * LANE, (hh + 1) * LANE)
    r_cols = lambda hh: slice((hh // 2) * LANE, (hh // 2 + 1) * LANE)
    chains = []
    for hh in heads:
        q = jnp.concatenate([qn_ref[0, :, h_cols(hh)], qr_ref[0, :, r_cols(hh)]], axis=1)
        for c in range(t // MXU_DIM):
            chains.append((hh, c, q[c * MXU_DIM:(c + 1) * MXU_DIM, :]))

    def keys(hh, start, size):
        half = slice((hh % 2) * LANE, (hh % 2 + 1) * LANE)
        return jnp.concatenate([kn_ref[0, pl.ds(start, size), h_cols(hh)],
                                kr_ref[0, pl.ds(start, size), half]], axis=1)

    def values_t(hh, start, size):
        return vt_ref[hh * V_HEAD:(hh + 1) * V_HEAD, pl.ds(start, size)]

    def frame_start(j):
        return pl.multiple_of(base + j * t, LANE)

    def issue_scores(j, buf):
        start = frame_start(j)
        for n, (hh, c, q) in enumerate(chains):
            s_sc[buf, n] = _dot_nt(keys(hh, start, t), q)

    def softmax_pv(n, s, v_t, first=False):
        m_new = jnp.max(s, axis=0, keepdims=True)
        if first:
            p = jnp.exp2(s - m_new)
            l_sc[n] = jnp.sum(p, axis=0, keepdims=True)
            acc_sc[n] = _dot(v_t, p.astype(BF16))
        else:
            m_prev = m_sc[n]
            m_new = jnp.maximum(m_prev, m_new)
            alpha = jnp.exp2(m_prev - m_new)
            p = jnp.exp2(s - m_new)
            l_sc[n] = alpha * l_sc[n] + jnp.sum(p, axis=0, keepdims=True)
            acc_sc[n] = alpha * acc_sc[n] + _dot(v_t, p.astype(BF16))
        m_sc[n] = m_new

    def consume(j, buf, mask=None):
        start = frame_start(j)
        for n, (hh, c, _) in enumerate(chains):
            s = s_sc[buf, n]
            if mask is not None:
                s = jnp.where(mask(c), s, NEG_BIG)
            softmax_pv(n, s, values_t(hh, start, t))

    meta_valid = lax.broadcasted_iota(jnp.int32, (base, 1), 0) >= PAD
    meta_scores = [_dot_nt(keys(hh, 0, base), q) for hh, c, q in chains]
    issue_scores(0, 0)
    for n, (hh, c, _) in enumerate(chains):
        softmax_pv(n, jnp.where(meta_valid, meta_scores[n], NEG_BIG), values_t(hh, 0, base), first=True)

    def pair(jj, carry):
        j = 2 * jj
        issue_scores(j + 1, 1)
        consume(j, 0)
        issue_scores(j + 2, 0)
        consume(j + 1, 1)
        return carry

    lax.fori_loop(0, i // 2, pair, 0)

    def diag_mask(c):
        kk = lax.broadcasted_iota(jnp.int32, (t, 1), 0)
        qq = c * MXU_DIM + lax.broadcasted_iota(jnp.int32, (1, MXU_DIM), 1)
        return kk < (qq // CHUNK + 1) * CHUNK

    @pl.when(i % 2 == 1)
    def _():
        issue_scores(i, 1)
        consume(i - 1, 0)
        consume(i, 1, diag_mask)

    @pl.when(i % 2 == 0)
    def _():
        consume(i, 0, diag_mask)

    row0 = frame_start(i)
    for n, (hh, c, _) in enumerate(chains):
        o_t = acc_sc[n] / l_sc[n]
        rows = pl.ds(pl.multiple_of(row0 + c * MXU_DIM, LANE), MXU_DIM)
        o_ref[0, rows, h_cols(hh)] = o_t.T.astype(BF16)

    @pl.when(i == 0)
    def _():
        for hh in heads:
            qm = jnp.concatenate([qnm_ref[0, :, h_cols(hh)], qrm_ref[0, :, r_cols(hh)]], axis=1)
            s = jnp.where(meta_valid, _dot_nt(keys(hh, 0, base), qm), NEG_BIG)
            p = jnp.exp2(s - jnp.max(s, axis=0, keepdims=True))
            o_t = _dot(values_t(hh, 0, base), p.astype(BF16)) / jnp.sum(p, axis=0, keepdims=True)
            o_ref[0, 0:base, h_cols(hh)] = o_t.T.astype(BF16)


def _attention(qn, qr, kn, kr, vt):
    batch = qn.shape[0]
    t = ATT_TILE
    base = PAD + N_META
    gw = HEAD_GROUP * LANE
    n_chain = HEAD_GROUP * (t // MXU_DIM)

    def frames(width):
        shape = (pl.Element(1), pl.Element(t), pl.Element(width))
        return pl.BlockSpec(shape, lambda b, g, i: (b, pl.multiple_of(base + i * t, LANE),
                                                    pl.multiple_of(g * width, LANE)))

    return pl.pallas_call(
        _attn_kernel,
        grid=(batch, MLA_HEADS // HEAD_GROUP, SEQ // t),
        in_specs=[frames(gw), frames(gw // 2),
                  pl.BlockSpec((1, base, gw), lambda b, g, i: (b, 0, g)),
                  pl.BlockSpec((1, base, gw // 2), lambda b, g, i: (b, 0, g)),
                  pl.BlockSpec((1, TP, gw), lambda b, g, i: (b, 0, g)),
                  pl.BlockSpec((1, TP, 2 * LANE), lambda b, g, i: (b, 0, 0)),
                  pl.BlockSpec((gw, TP), lambda b, g, i: (g, b))],
        out_specs=pl.BlockSpec((1, TP, gw), lambda b, g, i: (b, 0, g)),
        out_shape=jax.ShapeDtypeStruct((batch, TP, MLA_HEADS * V_HEAD), BF16),
        scratch_shapes=[pltpu.VMEM((2, n_chain, t, MXU_DIM), F32),
                        pltpu.VMEM((n_chain, 1, MXU_DIM), F32),
                        pltpu.VMEM((n_chain, 1, MXU_DIM), F32),
                        pltpu.VMEM((n_chain, V_HEAD, MXU_DIM), F32)],
        compiler_params=_params("parallel", "parallel", "arbitrary"),
        name="mla_attention",
    )(qn, qr, qn, qr, kn, kr, vt)


def _proj_residual_kernel(h_ref, a_ref, w_ref, o_ref):
    o_ref[0] = h_ref[0] + _dot(a_ref[0], w_ref[...])


def _proj_residual(h, a, w):
    batch = h.shape[0]
    tm = SEQ_TILE
    blk = lambda width: pl.BlockSpec((1, tm, width), lambda b, t: (b, t, 0))
    return pl.pallas_call(
        _proj_residual_kernel,
        grid=(batch, TP // tm),
        in_specs=[blk(D_MODEL), blk(a.shape[2]), _const_spec(w.shape)],
        out_specs=blk(D_MODEL),
        out_shape=jax.ShapeDtypeStruct(h.shape, F32),
        compiler_params=_params("parallel", "parallel"),
        name="proj_residual",
    )(h, a, w)


def _ffn_kernel(h_ref, g_ref, wg_ref, wu_ref, wd_ref, o_ref, hn_sc):
    @pl.when(pl.program_id(1) == 0)
    def _():
        x = h_ref[...]
        hn_sc[...] = _rms(x, g_ref[...]).astype(BF16)
        o_ref[...] = x

    hn = hn_sc[...]
    gate = _dot(hn, wg_ref[...])
    up = _dot(hn, wu_ref[...])
    act = (gate * jax.nn.sigmoid(gate) * up).astype(BF16)
    o_ref[...] += _dot(act, wd_ref[...])


def _ffn(h, g, w_gu, w_down):
    rows = h.shape[0]
    tm = FFN_ROW_TILE
    n_f = D_FF // FF_TILE
    return pl.pallas_call(
        _ffn_kernel,
        grid=(rows // tm, n_f),
        in_specs=[pl.BlockSpec((tm, D_MODEL), lambda i, j: (i, 0)),
                  pl.BlockSpec((1, D_MODEL), lambda i, j: (0, 0)),
                  pl.BlockSpec((D_MODEL, FF_TILE), lambda i, j: (0, j)),
                  pl.BlockSpec((D_MODEL, FF_TILE), lambda i, j: (0, j + n_f)),
                  pl.BlockSpec((FF_TILE, D_MODEL), lambda i, j: (j, 0))],
        out_specs=pl.BlockSpec((tm, D_MODEL), lambda i, j: (i, 0)),
        out_shape=jax.ShapeDtypeStruct(h.shape, F32),
        scratch_shapes=[pltpu.VMEM((tm, D_MODEL), BF16)],
        compiler_params=_params("parallel", "arbitrary"),
        name="ffn",
    )(h, g, w_gu, w_gu, w_down)


def _lru_in_kernel(h_ref, g_ref, w_ref, o_ref):
    hn = _rms(h_ref[...], g_ref[...]).astype(BF16)
    y = _dot(hn, w_ref[...])

    @pl.when(pl.program_id(0) == 0)
    def _():
        o_ref[...] = y

    @pl.when(pl.program_id(0) == 1)
    def _():
        c = (2.0 / jnp.pi) ** 0.5
        o_ref[...] = 0.5 * y * (1.0 + jnp.tanh(c * (y + 0.044715 * (y * y * y))))


def _lru_in(h, g, w):
    rows = h.shape[0]
    tm = ROW_TILE
    return pl.pallas_call(
        _lru_in_kernel,
        grid=(2, rows // tm),
        in_specs=[pl.BlockSpec((tm, D_MODEL), lambda n, i: (i, 0)), _const_spec(g.shape),
                  pl.BlockSpec((D_MODEL, D_RNN), lambda n, i: (0, n))],
        out_specs=pl.BlockSpec((tm, D_RNN), lambda n, i: (i, n)),
        out_shape=jax.ShapeDtypeStruct((rows, 2 * D_RNN), F32),
        compiler_params=_params("arbitrary", "parallel"),
        name="lru_in",
    )(h, g, w)


def _lru_kernel(xb_ref, yg_ref, cw_ref, cb_ref, wga_ref, bga_ref, wgx_ref, bgx_ref, lam_ref,
                o_ref, xpad_sc, a_sc, b_sc, h_sc):
    t = pl.program_id(1)
    tt = SEQ_TILE
    halo = 8

    @pl.when(t == 0)
    def _():
        xpad_sc[0:halo, :] = jnp.zeros((halo, D_RNN), F32)
        h_sc[...] = jnp.zeros(h_sc.shape, F32)

    row = t * tt + lax.broadcasted_iota(jnp.int32, (tt, 1), 0)
    valid = row >= PAD
    xpad_sc[halo:halo + tt, :] = jnp.where(valid, xb_ref[0], 0.0)

    lam = lam_ref[...]
    neg_lam = -lam
    softplus = jnp.maximum(neg_lam, 0.0) + jnp.log1p(jnp.exp(-jnp.abs(neg_lam)))
    for g in range(D_RNN // GATE_TILE):
        cols = slice(g * GATE_TILE, (g + 1) * GATE_TILE)
        xc = cb_ref[:, cols]
        for k in range(CONV_W):
            o = halo - (CONV_W - 1) + k
            xc = xc + cw_ref[k:k + 1, cols] * xpad_sc[o:o + tt, cols]
        xc16 = xc.astype(BF16)
        r = jax.nn.sigmoid(_dot(xc16, wga_ref[g]) + bga_ref[:, cols])
        ig = jax.nn.sigmoid(_dot(xc16, wgx_ref[g]) + bgx_ref[:, cols])
        log_a = -LRU_C * r * softplus[:, cols]
        a = jnp.exp(log_a)
        a_sc[:, cols] = a
        bb = jnp.sqrt(jnp.tanh(-log_a) * (1.0 + a * a)) * (ig * xc)
        b_sc[:, cols] = jnp.where(valid, bb, 0.0)

    xpad_sc[0:halo, :] = xpad_sc[tt:tt + halo, :]

    def body(s, h):
        h = a_sc[pl.ds(s, 1), :] * h + b_sc[pl.ds(s, 1), :]
        b_sc[pl.ds(s, 1), :] = h
        return h

    h_sc[...] = lax.fori_loop(0, tt, body, h_sc[...], unroll=8)
    o_ref[0] = (b_sc[...] * yg_ref[0]).astype(BF16)


def _lru(xy, cw, cb, wga, bga, wgx, bgx, lam):
    batch = xy.shape[0]
    tt = SEQ_TILE
    half = lambda n: pl.BlockSpec((1, tt, D_RNN), lambda b, t: (b, t, n))
    consts = (cw, cb, wga, bga, wgx, bgx, lam)
    return pl.pallas_call(
        _lru_kernel,
        grid=(batch, TP // tt),
        in_specs=[half(0), half(1)] + [_const_spec(c.shape) for c in consts],
        out_specs=half(0),
        out_shape=jax.ShapeDtypeStruct((batch, TP, D_RNN), BF16),
        scratch_shapes=[pltpu.VMEM((tt + 8, D_RNN), F32), pltpu.VMEM((tt, D_RNN), F32),
                        pltpu.VMEM((tt, D_RNN), F32), pltpu.VMEM((1, D_RNN), F32)],
        compiler_params=_params("parallel", "arbitrary"),
        name="rglru",
    )(xy, xy, *consts)


def _final_norm_kernel(h_ref, g_ref, o_ref):
    o_ref[0] = _rms(h_ref[0], g_ref[...])


def _final_norm(h, g):
    batch = h.shape[0]
    return pl.pallas_call(
        _final_norm_kernel,
        grid=(batch, SEQ // ROW_TILE),
        in_specs=[pl.BlockSpec((pl.Element(1), pl.Element(ROW_TILE), pl.Element(D_MODEL)),
                               lambda b, i: (b, pl.multiple_of(PAD + N_META + i * ROW_TILE, LANE), 0)),
                  _const_spec(g.shape)],
        out_specs=pl.BlockSpec((1, ROW_TILE, D_MODEL), lambda b, i: (b, i, 0)),
        out_shape=jax.ShapeDtypeStruct((batch, SEQ, D_MODEL), F32),
        compiler_params=_params("parallel", "parallel"),
        name="final_norm",
    )(h, g)


def _rot_half_cols(w):
    return jnp.roll(w, -QK_ROPE // 2, axis=-1)


def _mla_weights(w_in, w_uq, w_ukv):
    w_q = w_in[:, :Q_LORA]
    w_kv = w_in[:, Q_LORA:Q_LORA + KV_LORA]
    w_kr = w_in[:, Q_LORA + KV_LORA:]
    z = jnp.zeros_like(w_kr)
    pair = lambda w: [w, z, z, w]
    win = jnp.concatenate([w_q, w_kv] + pair(w_kr) + pair(_rot_half_cols(w_kr)), axis=1)
    uq = w_uq.reshape(Q_LORA, MLA_HEADS, QK_NOPE + QK_ROPE)
    uq_n = uq[:, :, :QK_NOPE].reshape(Q_LORA, -1)
    uq_r = uq[:, :, QK_NOPE:]
    wuq = jnp.concatenate([uq_n, uq_r.reshape(Q_LORA, -1),
                           _rot_half_cols(uq_r).reshape(Q_LORA, -1)], axis=1)
    ukv = w_ukv.reshape(KV_LORA, MLA_HEADS, QK_NOPE + V_HEAD)
    wuk = ukv[:, :, :QK_NOPE].reshape(KV_LORA, -1)
    wvt = ukv[:, :, QK_NOPE:].reshape(KV_LORA, -1).T
    return win.astype(BF16), wuq.astype(BF16), wuk.astype(BF16), wvt.astype(BF16)


def _gate_block_diag(w):
    w = w.reshape(RNN_BLOCKS // 2, 2, RNN_BW, RNN_BW)
    z = jnp.zeros_like(w[:, 0])
    top = jnp.concatenate([w[:, 0], z], axis=2)
    bot = jnp.concatenate([z, w[:, 1]], axis=2)
    return jnp.concatenate([top, bot], axis=1).astype(BF16)


def _rope_tables():
    pos = jnp.maximum(jnp.arange(TP) - PAD, 0).astype(F32)
    inv_freq = ROPE_THETA ** (-jnp.arange(0, QK_ROPE, 2, dtype=F32) / QK_ROPE)
    ang = pos[:, None] * inv_freq[None, :]
    cos, sin = jnp.cos(ang), jnp.sin(ang)
    cos = jnp.concatenate([cos, cos, cos, cos], axis=1)
    sin = jnp.concatenate([-sin, sin, -sin, sin], axis=1)
    return cos, sin


def _mla_layer(h, g_mix, w_in, qg, kvg, w_uq, w_ukv, w_o, cos, sin):
    win, wuq, wuk, wvt = _mla_weights(w_in, w_uq, w_ukv)
    qn, qr, kn, kr, vt = _mla_proj(h, g_mix, win, qg, kvg, wuq, wuk, wvt, cos, sin)
    return _proj_residual(h, _attention(qn, qr, kn, kr, vt), w_o.astype(BF16))


def _lru_layer(h, g_mix, w_in, cw, cb, w_ga, b_ga, w_gx, b_gx, lam, w_o):
    batch = h.shape[0]
    xy = _lru_in(h.reshape(batch * TP, D_MODEL), g_mix, w_in.astype(BF16))
    mixed = _lru(xy.reshape(batch, TP, 2 * D_RNN), cw, cb, _gate_block_diag(w_ga), b_ga,
                 _gate_block_diag(w_gx), b_gx, lam)
    return _proj_residual(h, mixed, w_o.astype(BF16))


def kernel(x, meta_tokens, norm_mix, norm_ffn, norm_final, mla_w_in, mla_q_norm, mla_kv_norm, mla_w_uq, mla_w_ukv, mla_w_o, lru_w_in, lru_conv_w, lru_conv_b, lru_w_gate_a, lru_b_gate_a, lru_w_gate_x, lru_b_gate_x, lru_lambda, lru_w_o, ffn_w_gu, ffn_w_down):
    batch = x.shape[0]
    rows = batch * TP
    meta = jnp.broadcast_to(meta_tokens.astype(x.dtype)[None], (batch, N_META, D_MODEL))
    h = jnp.concatenate([jnp.zeros((batch, PAD, D_MODEL), x.dtype), meta, x], axis=1)
    cos, sin = _rope_tables()
    row_vec = lambda v: v.reshape(1, -1)
    flat = lambda a: a.reshape(rows, a.shape[-1])
    seq = lambda a: a.reshape(batch, TP, a.shape[-1])

    for layer in range(DEPTH):
        j = layer // N_MIXERS
        g_mix = row_vec(norm_mix[layer])
        if layer % N_MIXERS == 0:
            h = _mla_layer(h, g_mix, mla_w_in[j], row_vec(mla_q_norm[j]), row_vec(mla_kv_norm[j]),
                           mla_w_uq[j], mla_w_ukv[j], mla_w_o[j], cos, sin)
        else:
            h = _lru_layer(h, g_mix, lru_w_in[j], lru_conv_w[j], row_vec(lru_conv_b[j]),
                           lru_w_gate_a[j], row_vec(lru_b_gate_a[j]), lru_w_gate_x[j],
                           row_vec(lru_b_gate_x[j]), row_vec(lru_lambda[j]), lru_w_o[j])
        h = seq(_ffn(flat(h), row_vec(norm_ffn[layer]), ffn_w_gu[layer].astype(BF16),
                     ffn_w_down[layer].astype(BF16)))
    return _final_norm(h, row_vec(norm_final))
```

```python
import math

import jax
import jax.numpy as jnp
from jax import lax
from jax.experimental import pallas as pl
from jax.experimental.pallas import tpu as pltpu

D_MODEL = 2048
SEQ = 4096
CHUNK = 64
N_META = 16
DEPTH = 4
N_MIXERS = 2

MLA_HEADS = 16
Q_LORA = 512
KV_LORA = 512
QK_NOPE = 128
QK_ROPE = 64
V_HEAD = 128
ROPE_THETA = 10000.0

D_RNN = D_MODEL
RNN_BLOCKS = 16
RNN_BW = D_RNN // RNN_BLOCKS
CONV_W = 4
LRU_C = 8.0

D_FF = -(-8 * D_MODEL // (3 * 256)) * 256

RMS_EPS = 1e-6
NEG_BIG = -1e30

LANE = 128
MXU_DIM = 256
PAD = LANE - N_META
TP = PAD + N_META + SEQ
VMEM_LIMIT = 56 * 1024 * 1024

ROW_TILE = 512
SEQ_TILE = 384
LRU_IN_TILE = 512
FFN_ROW_TILE = 768
FF_TILE = 512
ATT_TILE = 512
HEAD_GROUP = 4
L_ROWS = 16
GATE_TILE = 2 * RNN_BW
SCAN_BLOCK = 8

BF16 = jnp.bfloat16
F32 = jnp.float32


def _params(*sem):
    return pltpu.CompilerParams(dimension_semantics=sem, vmem_limit_bytes=VMEM_LIMIT)


def _rms(x, g):
    ms = jnp.mean(x * x, axis=-1, keepdims=True)
    return x * lax.rsqrt(ms + RMS_EPS) * g


def _dot(a, b):
    return jnp.dot(a, b, preferred_element_type=F32)


def _dot_nt(a, b):
    return lax.dot_general(a, b, (((1,), (1,)), ((), ())), preferred_element_type=F32)


def _const_spec(shape):
    return pl.BlockSpec(shape, lambda *_: (0,) * len(shape))


def _layer_spec(shape, layer):
    return pl.BlockSpec((None,) + tuple(shape), lambda *_: (layer,) + (0,) * len(shape))


def _mla_proj_kernel(h_ref, g_ref, win_ref, qg_ref, kvg_ref, wuq_ref, wuk_ref, wvt_ref,
                     cos_ref, sin_ref, qn_ref, qr_ref, kn_ref, kr_ref, vt_ref):
    scale = (QK_NOPE + QK_ROPE) ** -0.5 * math.log2(math.e)
    hn = _rms(h_ref[0], g_ref[...]).astype(BF16)
    proj = _dot(hn, win_ref[...])
    cq = _rms(proj[:, :Q_LORA], qg_ref[...]).astype(BF16)
    ckv = _rms(proj[:, Q_LORA:Q_LORA + KV_LORA], kvg_ref[...]).astype(BF16)
    cos = cos_ref[...]
    sin = sin_ref[...]
    o = Q_LORA + KV_LORA
    for p in range(2):
        raw = proj[:, o + p * LANE:o + (p + 1) * LANE]
        rot = proj[:, o + (2 + p) * LANE:o + (3 + p) * LANE]
        kr_ref[0, :, p * LANE:(p + 1) * LANE] = (raw * cos + rot * sin).astype(BF16)
    q = _dot(cq, wuq_ref[...])
    hn_w = MLA_HEADS * QK_NOPE
    hr_w = MLA_HEADS * QK_ROPE
    qn_ref[0] = (q[:, :hn_w] * scale).astype(BF16)
    for p in range(hr_w // LANE):
        raw = q[:, hn_w + p * LANE:hn_w + (p + 1) * LANE]
        rot = q[:, hn_w + hr_w + p * LANE:hn_w + hr_w + (p + 1) * LANE]
        qr_ref[0, :, p * LANE:(p + 1) * LANE] = ((raw * cos + rot * sin) * scale).astype(BF16)
    kn_ref[0] = _dot(ckv, wuk_ref[...]).astype(BF16)
    vt_ref[...] = _dot_nt(wvt_ref[...], ckv).astype(BF16)


def _mla_proj(h, g, win, qg, kvg, wuq, wuk, wvt, cos, sin):
    batch = h.shape[0]
    tm = SEQ_TILE
    n_t = TP // tm
    hn_w = MLA_HEADS * QK_NOPE
    hr_w = MLA_HEADS * QK_ROPE
    seq_blk = lambda w: pl.BlockSpec((1, tm, w), lambda b, t: (b, t, 0))
    consts = (g, win, qg, kvg, wuq, wuk, wvt)
    out_w = (hn_w, hr_w, hn_w, 2 * LANE)
    return pl.pallas_call(
        _mla_proj_kernel,
        grid=(batch, n_t),
        in_specs=[seq_blk(D_MODEL)] + [_const_spec(c.shape) for c in consts]
        + [pl.BlockSpec((tm, LANE), lambda b, t: (t, 0))] * 2,
        out_specs=[seq_blk(w) for w in out_w]
        + [pl.BlockSpec((hn_w, tm), lambda b, t: (0, b * n_t + t))],
        out_shape=[jax.ShapeDtypeStruct((batch, TP, w), BF16) for w in out_w]
        + [jax.ShapeDtypeStruct((hn_w, batch * TP), BF16)],
        compiler_params=_params("parallel", "parallel"),
        name="mla_proj",
    )(h, *consts, cos, sin)


def _attn_kernel(qn_ref, qr_ref, qnm_ref, qrm_ref, kn_ref, kr_ref, vt_ref, o_ref,
                 s_sc, smax_sc, m_sc, acc_sc):
    i = pl.program_id(2)
    t = ATT_TILE
    base = PAD + N_META
    heads = range(HEAD_GROUP)
    h_cols = lambda hh: slice(hh * LANE, (hh + 1) * LANE)
    r_cols = lambda hh: slice((hh // 2) * LANE, (hh // 2 + 1) * LANE)
    chains = []
    for hh in heads:
        q = jnp.concatenate([qn_ref[0, :, h_cols(hh)], qr_ref[0, :, r_cols(hh)]], axis=1)
        for c in range(t // MXU_DIM):
            chains.append((hh, c, q[c * MXU_DIM:(c + 1) * MXU_DIM, :]))

    def keys(hh, start, size):
        half = slice((hh % 2) * LANE, (hh % 2 + 1) * LANE)
        return jnp.concatenate([kn_ref[0, pl.ds(start, size), h_cols(hh)],
                                kr_ref[0, pl.ds(start, size), half]], axis=1)

    def values_t(hh, start, size):
        return jnp.concatenate([vt_ref[hh * V_HEAD:(hh + 1) * V_HEAD, pl.ds(start, size)],
                                jnp.ones((L_ROWS, size), BF16)], axis=0)

    def frame_start(j):
        return pl.multiple_of(base + j * t, LANE)

    def issue_scores(j, buf):
        start = frame_start(j)
        for n, (hh, c, q) in enumerate(chains):
            s = _dot_nt(keys(hh, start, t), q)
            s_sc[buf, n] = s
            smax_sc[buf, n] = jnp.max(s, axis=0, keepdims=True)

    def softmax_pv(n, s, v_t, first=False, s_max=None):
        m_new = jnp.max(s, axis=0, keepdims=True) if s_max is None else s_max
        if first:
            p = jnp.exp2(s - m_new)
            acc_sc[n] = _dot(v_t, p.astype(BF16))
        else:
            m_prev = m_sc[n]
            m_new = jnp.maximum(m_prev, m_new)
            alpha = jnp.exp2(m_prev - m_new)
            p = jnp.exp2(s - m_new)
            acc_sc[n] = alpha * acc_sc[n] + _dot(v_t, p.astype(BF16))
        m_sc[n] = m_new

    def consume(j, buf, mask=None):
        start = frame_start(j)
        for n, (hh, c, _) in enumerate(chains):
            s = s_sc[buf, n]
            s_max = None
            if mask is not None:
                s = jnp.where(mask(c), s, NEG_BIG)
            else:
                s_max = smax_sc[buf, n]
            softmax_pv(n, s, values_t(hh, start, t), s_max=s_max)

    meta_valid = lax.broadcasted_iota(jnp.int32, (base, 1), 0) >= PAD
    meta_scores = [_dot_nt(keys(hh, 0, base), q) for hh, c, q in chains]
    issue_scores(0, 0)
    for n, (hh, c, _) in enumerate(chains):
        softmax_pv(n, jnp.where(meta_valid, meta_scores[n], NEG_BIG), values_t(hh, 0, base), first=True)

    def pair(jj, carry):
        j = 2 * jj
        issue_scores(j + 1, 1)
        consume(j, 0)
        issue_scores(j + 2, 0)
        consume(j + 1, 1)
        return carry

    lax.fori_loop(0, i // 2, pair, 0)

    def diag_mask(c):
        kk = lax.broadcasted_iota(jnp.int32, (t, 1), 0)
        qq = c * MXU_DIM + lax.broadcasted_iota(jnp.int32, (1, MXU_DIM), 1)
        return kk < (qq // CHUNK + 1) * CHUNK

    @pl.when(i % 2 == 1)
    def _():
        issue_scores(i, 1)
        consume(i - 1, 0)
        consume(i, 1, diag_mask)

    @pl.when(i % 2 == 0)
    def _():
        consume(i, 0, diag_mask)

    row0 = frame_start(i)
    for n, (hh, c, _) in enumerate(chains):
        o_t = acc_sc[n, 0:V_HEAD] / acc_sc[n, V_HEAD:V_HEAD + 1]
        rows = pl.ds(pl.multiple_of(row0 + c * MXU_DIM, LANE), MXU_DIM)
        o_ref[0, rows, h_cols(hh)] = o_t.T.astype(BF16)

    @pl.when(i == 0)
    def _():
        for hh in heads:
            qm = jnp.concatenate([qnm_ref[0, :, h_cols(hh)], qrm_ref[0, :, r_cols(hh)]], axis=1)
            s = jnp.where(meta_valid, _dot_nt(keys(hh, 0, base), qm), NEG_BIG)
            p = jnp.exp2(s - jnp.max(s, axis=0, keepdims=True))
            pv = _dot(values_t(hh, 0, base), p.astype(BF16))
            o_t = pv[0:V_HEAD] / pv[V_HEAD:V_HEAD + 1]
            o_ref[0, 0:base, h_cols(hh)] = o_t.T.astype(BF16)


def _attention(qn, qr, kn, kr, vt):
    batch = qn.shape[0]
    t = ATT_TILE
    base = PAD + N_META
    gw = HEAD_GROUP * LANE
    n_chain = HEAD_GROUP * (t // MXU_DIM)

    def frames(width):
        shape = (pl.Element(1), pl.Element(t), pl.Element(width))
        return pl.BlockSpec(shape, lambda b, g, i: (b, pl.multiple_of(base + i * t, LANE),
                                                    pl.multiple_of(g * width, LANE)))

    return pl.pallas_call(
        _attn_kernel,
        grid=(batch, MLA_HEADS // HEAD_GROUP, SEQ // t),
        in_specs=[frames(gw), frames(gw // 2),
                  pl.BlockSpec((1, base, gw), lambda b, g, i: (b, 0, g)),
                  pl.BlockSpec((1, base, gw // 2), lambda b, g, i: (b, 0, g)),
                  pl.BlockSpec((1, TP, gw), lambda b, g, i: (b, 0, g)),
                  pl.BlockSpec((1, TP, 2 * LANE), lambda b, g, i: (b, 0, 0)),
                  pl.BlockSpec((gw, TP), lambda b, g, i: (g, b))],
        out_specs=pl.BlockSpec((1, TP, gw), lambda b, g, i: (b, 0, g)),
        out_shape=jax.ShapeDtypeStruct((batch, TP, MLA_HEADS * V_HEAD), BF16),
        scratch_shapes=[pltpu.VMEM((2, n_chain, t, MXU_DIM), F32),
                        pltpu.VMEM((2, n_chain, 1, MXU_DIM), F32),
                        pltpu.VMEM((n_chain, 1, MXU_DIM), F32),
                        pltpu.VMEM((n_chain, V_HEAD + L_ROWS, MXU_DIM), F32)],
        compiler_params=_params("parallel", "parallel", "arbitrary"),
        name="mla_attention",
    )(qn, qr, qn, qr, kn, kr, vt)


def _proj_residual_kernel(h_ref, a_ref, w_ref, o_ref):
    o_ref[0] = h_ref[0] + _dot(a_ref[0], w_ref[...])


def _proj_residual(h, a, w, layer):
    batch = h.shape[0]
    tm = SEQ_TILE
    blk = lambda width: pl.BlockSpec((1, tm, width), lambda b, t: (b, t, 0))
    return pl.pallas_call(
        _proj_residual_kernel,
        grid=(batch, TP // tm),
        in_specs=[blk(D_MODEL), blk(a.shape[2]), _layer_spec(w.shape[1:], layer)],
        out_specs=blk(D_MODEL),
        out_shape=jax.ShapeDtypeStruct(h.shape, F32),
        compiler_params=_params("parallel", "parallel"),
        name="proj_residual",
    )(h, a, w)


def _ffn_kernel(h_ref, g_ref, wg_ref, wu_ref, wd_ref, o_ref, hn_sc):
    @pl.when(pl.program_id(1) == 0)
    def _():
        x = h_ref[...]
        hn_sc[...] = _rms(x, g_ref[...]).astype(BF16)
        o_ref[...] = x

    hn = hn_sc[...]
    gate = _dot(hn, wg_ref[...])
    up = _dot(hn, wu_ref[...])
    act = (gate * jax.nn.sigmoid(gate) * up).astype(BF16)
    o_ref[...] += _dot(act, wd_ref[...])


def _ffn(h, g, w_gu, w_down, layer):
    rows = h.shape[0]
    tm = FFN_ROW_TILE
    n_f = D_FF // FF_TILE
    return pl.pallas_call(
        _ffn_kernel,
        grid=(rows // tm, n_f),
        in_specs=[pl.BlockSpec((tm, D_MODEL), lambda i, j: (i, 0)),
                  pl.BlockSpec((1, D_MODEL), lambda i, j: (0, 0)),
                  pl.BlockSpec((None, D_MODEL, FF_TILE), lambda i, j: (layer, 0, j)),
                  pl.BlockSpec((None, D_MODEL, FF_TILE), lambda i, j: (layer, 0, j + n_f)),
                  pl.BlockSpec((None, FF_TILE, D_MODEL), lambda i, j: (layer, j, 0))],
        out_specs=pl.BlockSpec((tm, D_MODEL), lambda i, j: (i, 0)),
        out_shape=jax.ShapeDtypeStruct(h.shape, F32),
        scratch_shapes=[pltpu.VMEM((tm, D_MODEL), BF16)],
        compiler_params=_params("parallel", "arbitrary"),
        name="ffn",
    )(h, g, w_gu, w_gu, w_down)


def _lru_in_kernel(h_ref, g_ref, w_ref, xb_ref, yg_ref):
    half = LRU_IN_TILE // 2
    for r in range(2):
        rows = slice(r * half, (r + 1) * half)
        hn = _rms(h_ref[rows, :], g_ref[...]).astype(BF16)
        xb_ref[rows, :] = _dot(hn, w_ref[:, :D_RNN])
        y = _dot(hn, w_ref[:, D_RNN:])
        c = (2.0 / jnp.pi) ** 0.5
        yg_ref[rows, :] = 0.5 * y * (1.0 + jnp.tanh(c * (y + 0.044715 * (y * y * y))))


def _lru_in(h, g, w, layer):
    rows = h.shape[0]
    tm = LRU_IN_TILE
    row = lambda width: pl.BlockSpec((tm, width), lambda i: (i, 0))
    return pl.pallas_call(
        _lru_in_kernel,
        grid=(rows // tm,),
        in_specs=[row(D_MODEL), _const_spec(g.shape), _layer_spec(w.shape[1:], layer)],
        out_specs=[row(D_RNN), row(D_RNN)],
        out_shape=[jax.ShapeDtypeStruct((rows, D_RNN), F32)] * 2,
        compiler_params=_params("parallel"),
        name="lru_in",
    )(h, g, w)


def _lru_kernel(xb_ref, yg_ref, cw_ref, cb_ref, wga_ref, bga_ref, wgx_ref, bgx_ref, lam_ref,
                o_ref, x_sc, tail_sc, a_sc, b_sc, hs_sc, h_sc):
    t = pl.program_id(1)
    tt = SEQ_TILE
    halo = 8

    x_sc[...] = xb_ref[0]

    @pl.when(t == 0)
    def _():
        x_sc[0:PAD, :] = jnp.zeros((PAD, D_RNN), F32)
        tail_sc[...] = jnp.zeros(tail_sc.shape, F32)
        h_sc[...] = jnp.zeros(h_sc.shape, F32)

    neg_lam = -lam_ref[...]
    softplus = jnp.maximum(neg_lam, 0.0) + jnp.log1p(jnp.exp(-jnp.abs(neg_lam)))
    neg_log_a_rate = LRU_C * softplus
    a_exp2_rate = -math.log2(math.e) * neg_log_a_rate
    for g in range(D_RNN // GATE_TILE):
        cols = slice(g * GATE_TILE, (g + 1) * GATE_TILE)
        x = x_sc[:, cols]
        taps = [cw_ref[k:k + 1, cols] for k in range(CONV_W)]
        xc = cb_ref[:, cols] + taps[CONV_W - 1] * x
        for d in range(1, CONV_W):
            xc = xc + taps[CONV_W - 1 - d] * pltpu.roll(x, d, axis=0)
        head = jnp.concatenate([tail_sc[:, cols], x[0:halo]], axis=0)
        xc_head = cb_ref[:, cols]
        for k in range(CONV_W):
            o = halo - (CONV_W - 1) + k
            xc_head = xc_head + taps[k] * head[o:o + halo]
        xc = jnp.concatenate([xc_head, xc[halo:]], axis=0)
        xc16 = xc.astype(BF16)
        r = jax.nn.sigmoid(_dot(xc16, wga_ref[g]) + bga_ref[:, cols])
        ig = jax.nn.sigmoid(_dot(xc16, wgx_ref[g]) + bgx_ref[:, cols])
        a = jnp.exp2(r * a_exp2_rate[:, cols])
        a_sc[:, :, cols] = a.reshape(tt // SCAN_BLOCK, SCAN_BLOCK, GATE_TILE)
        gain = jnp.sqrt(jnp.tanh(r * neg_log_a_rate[:, cols]) * (1.0 + a * a))
        b_sc[:, :, cols] = (gain * (ig * xc)).reshape(tt // SCAN_BLOCK, SCAN_BLOCK, GATE_TILE)

    tail_sc[...] = x_sc[tt - halo:tt, :]

    @pl.when(t == 0)
    def _():
        b_sc[0:PAD // SCAN_BLOCK] = jnp.zeros((PAD // SCAN_BLOCK, SCAN_BLOCK, D_RNN), F32)

    def body(blk, h):
        a_cum = b_cum = h_out = None
        for k in range(SCAN_BLOCK):
            a_k = a_sc[blk, k:k + 1, :]
            b_k = b_sc[blk, k:k + 1, :]
            if k == 0:
                a_cum, b_cum = a_k, b_k
            else:
                a_cum, b_cum = a_k * a_cum, a_k * b_cum + b_k
            h_out = a_cum * h + b_cum
            hs_sc[blk, k:k + 1, :] = h_out
        return h_out

    h_sc[...] = lax.fori_loop(0, tt // SCAN_BLOCK, body, h_sc[...], unroll=4)
    o_ref[0] = (hs_sc[...].reshape(tt, D_RNN) * yg_ref[0]).astype(BF16)


def _lru(xb, yg, cw, cb, wga, bga, wgx, bgx, lam):
    batch = xb.shape[0]
    tt = SEQ_TILE
    blk = pl.BlockSpec((1, tt, D_RNN), lambda b, t: (b, t, 0))
    consts = (cw, cb, wga, bga, wgx, bgx, lam)
    return pl.pallas_call(
        _lru_kernel,
        grid=(batch, TP // tt),
        in_specs=[blk, blk] + [_const_spec(c.shape) for c in consts],
        out_specs=blk,
        out_shape=jax.ShapeDtypeStruct(xb.shape, BF16),
        scratch_shapes=[pltpu.VMEM((tt, D_RNN), F32), pltpu.VMEM((8, D_RNN), F32)]
        + [pltpu.VMEM((tt // SCAN_BLOCK, SCAN_BLOCK, D_RNN), F32)] * 3
        + [pltpu.VMEM((1, D_RNN), F32)],
        compiler_params=_params("parallel", "arbitrary"),
        name="rglru",
    )(xb, yg, *consts)


def _final_norm_kernel(h_ref, g_ref, o_ref):
    o_ref[0] = _rms(h_ref[0], g_ref[...])


def _final_norm(h, g):
    batch = h.shape[0]
    return pl.pallas_call(
        _final_norm_kernel,
        grid=(batch, SEQ // ROW_TILE),
        in_specs=[pl.BlockSpec((pl.Element(1), pl.Element(ROW_TILE), pl.Element(D_MODEL)),
                               lambda b, i: (b, pl.multiple_of(PAD + N_META + i * ROW_TILE, LANE), 0)),
                  _const_spec(g.shape)],
        out_specs=pl.BlockSpec((1, ROW_TILE, D_MODEL), lambda b, i: (b, i, 0)),
        out_shape=jax.ShapeDtypeStruct((batch, SEQ, D_MODEL), F32),
        compiler_params=_params("parallel", "parallel"),
        name="final_norm",
    )(h, g)


def _rot_half_cols(w):
    return jnp.roll(w, -QK_ROPE // 2, axis=-1)


def _mla_weights(w_in, w_uq, w_ukv):
    w_q = w_in[:, :Q_LORA]
    w_kv = w_in[:, Q_LORA:Q_LORA + KV_LORA]
    w_kr = w_in[:, Q_LORA + KV_LORA:]
    z = jnp.zeros_like(w_kr)
    pair = lambda w: [w, z, z, w]
    win = jnp.concatenate([w_q, w_kv] + pair(w_kr) + pair(_rot_half_cols(w_kr)), axis=1)
    uq = w_uq.reshape(Q_LORA, MLA_HEADS, QK_NOPE + QK_ROPE)
    uq_n = uq[:, :, :QK_NOPE].reshape(Q_LORA, -1)
    uq_r = uq[:, :, QK_NOPE:]
    wuq = jnp.concatenate([uq_n, uq_r.reshape(Q_LORA, -1),
                           _rot_half_cols(uq_r).reshape(Q_LORA, -1)], axis=1)
    ukv = w_ukv.reshape(KV_LORA, MLA_HEADS, QK_NOPE + V_HEAD)
    wuk = ukv[:, :, :QK_NOPE].reshape(KV_LORA, -1)
    wvt = ukv[:, :, QK_NOPE:].reshape(KV_LORA, -1).T
    return win.astype(BF16), wuq.astype(BF16), wuk.astype(BF16), wvt.astype(BF16)


def _gate_block_diag(w):
    w = w.reshape(RNN_BLOCKS // 2, 2, RNN_BW, RNN_BW)
    z = jnp.zeros_like(w[:, 0])
    top = jnp.concatenate([w[:, 0], z], axis=2)
    bot = jnp.concatenate([z, w[:, 1]], axis=2)
    return jnp.concatenate([top, bot], axis=1).astype(BF16)


def _rope_tables():
    pos = jnp.maximum(jnp.arange(TP) - PAD, 0).astype(F32)
    inv_freq = ROPE_THETA ** (-jnp.arange(0, QK_ROPE, 2, dtype=F32) / QK_ROPE)
    ang = pos[:, None] * inv_freq[None, :]
    cos, sin = jnp.cos(ang), jnp.sin(ang)
    cos = jnp.concatenate([cos, cos, cos, cos], axis=1)
    sin = jnp.concatenate([-sin, sin, -sin, sin], axis=1)
    return cos, sin


def _mla_layer(h, g_mix, w_in, qg, kvg, w_uq, w_ukv, w_o_all, layer, cos, sin):
    win, wuq, wuk, wvt = _mla_weights(w_in, w_uq, w_ukv)
    qn, qr, kn, kr, vt = _mla_proj(h, g_mix, win, qg, kvg, wuq, wuk, wvt, cos, sin)
    return _proj_residual(h, _attention(qn, qr, kn, kr, vt), w_o_all, layer)


def _lru_layer(h, g_mix, w_in_all, cw, cb, w_ga, b_ga, w_gx, b_gx, lam, w_o_all, layer):
    batch = h.shape[0]
    hf = h.reshape(batch * TP, D_MODEL)
    xb, yg = (a.reshape(batch, TP, D_RNN) for a in _lru_in(hf, g_mix, w_in_all, layer))
    mixed = _lru(xb, yg, cw, cb, _gate_block_diag(w_ga), b_ga, _gate_block_diag(w_gx), b_gx, lam)
    return _proj_residual(h, mixed, w_o_all, layer)


def kernel(x, meta_tokens, norm_mix, norm_ffn, norm_final, mla_w_in, mla_q_norm, mla_kv_norm, mla_w_uq, mla_w_ukv, mla_w_o, lru_w_in, lru_conv_w, lru_conv_b, lru_w_gate_a, lru_b_gate_a, lru_w_gate_x, lru_b_gate_x, lru_lambda, lru_w_o, ffn_w_gu, ffn_w_down):
    batch = x.shape[0]
    rows = batch * TP
    meta = jnp.broadcast_to(meta_tokens.astype(x.dtype)[None], (batch, N_META, D_MODEL))
    h = jnp.concatenate([jnp.zeros((batch, PAD, D_MODEL), x.dtype), meta, x], axis=1)
    cos, sin = _rope_tables()
    row_vec = lambda v: v.reshape(1, -1)
    flat = lambda a: a.reshape(rows, a.shape[-1])
    seq = lambda a: a.reshape(batch, TP, a.shape[-1])
    mla_w_o, lru_w_in, lru_w_o, ffn_w_gu, ffn_w_down = (
        w.astype(BF16) for w in (mla_w_o, lru_w_in, lru_w_o, ffn_w_gu, ffn_w_down))

    for layer in range(DEPTH):
        j = layer // N_MIXERS
        g_mix = row_vec(norm_mix[layer])
        if layer % N_MIXERS == 0:
            h = _mla_layer(h, g_mix, mla_w_in[j], row_vec(mla_q_norm[j]), row_vec(mla_kv_norm[j]),
                           mla_w_uq[j], mla_w_ukv[j], mla_w_o, j, cos, sin)
        else:
            h = _lru_layer(h, g_mix, lru_w_in, lru_conv_w[j], row_vec(lru_conv_b[j]),
                           lru_w_gate_a[j], row_vec(lru_b_gate_a[j]), lru_w_gate_x[j],
                           row_vec(lru_b_gate_x[j]), row_vec(lru_lambda[j]), lru_w_o, j)
        h = seq(_ffn(flat(h), row_vec(norm_ffn[layer]), ffn_w_gu, ffn_w_down, layer))
    return _final_norm(h, row_vec(norm_final))
```

```python
import math

import jax
import jax.numpy as jnp
from jax import lax
from jax.experimental import pallas as pl
from jax.experimental.pallas import tpu as pltpu

D_MODEL = 2048
SEQ = 4096
CHUNK = 64
N_META = 16
DEPTH = 4
N_MIXERS = 2

MLA_HEADS = 16
Q_LORA = 512
KV_LORA = 512
QK_NOPE = 128
QK_ROPE = 64
V_HEAD = 128
ROPE_THETA = 10000.0

D_RNN = D_MODEL
RNN_BLOCKS = 16
RNN_BW = D_RNN // RNN_BLOCKS
CONV_W = 4
LRU_C = 8.0

D_FF = -(-8 * D_MODEL // (3 * 256)) * 256

RMS_EPS = 1e-6
NEG_BIG = -1e30

LANE = 128
MXU_DIM = 256
PAD = LANE - N_META
TP = PAD + N_META + SEQ
VMEM_LIMIT = 56 * 1024 * 1024

ROW_TILE = 512
SEQ_TILE = 384
LRU_IN_TILE = 512
PROJ_ROW_TILE = 768
FFN_ROW_TILE = 1056
FF_TILE = 512
ATT_TILE = 512
HEAD_GROUP = 4
L_ROWS = 16
GATE_TILE = 2 * RNN_BW
SCAN_BLOCK = 8

BF16 = jnp.bfloat16
F32 = jnp.float32


def _params(*sem):
    return pltpu.CompilerParams(dimension_semantics=sem, vmem_limit_bytes=VMEM_LIMIT)


def _rms(x, g):
    ms = jnp.mean(x * x, axis=-1, keepdims=True)
    return x * lax.rsqrt(ms + RMS_EPS) * g


def _dot(a, b):
    return jnp.dot(a, b, preferred_element_type=F32)


def _dot_nt(a, b):
    return lax.dot_general(a, b, (((1,), (1,)), ((), ())), preferred_element_type=F32)


def _const_spec(shape):
    return pl.BlockSpec(shape, lambda *_: (0,) * len(shape))


def _layer_spec(shape, layer):
    return pl.BlockSpec((None,) + tuple(shape), lambda *_: (layer,) + (0,) * len(shape))


def _mla_proj_kernel(h_ref, g_ref, win_ref, qg_ref, kvg_ref, wuq_ref, wuk_ref, wvt_ref,
                     cos_ref, sin_ref, qn_ref, qr_ref, kn_ref, kr_ref, vt_ref):
    scale = (QK_NOPE + QK_ROPE) ** -0.5 * math.log2(math.e)
    hn = _rms(h_ref[0], g_ref[...]).astype(BF16)
    proj = _dot(hn, win_ref[...])
    cq = _rms(proj[:, :Q_LORA], qg_ref[...]).astype(BF16)
    ckv = _rms(proj[:, Q_LORA:Q_LORA + KV_LORA], kvg_ref[...]).astype(BF16)
    cos = cos_ref[...]
    sin = sin_ref[...]
    o = Q_LORA + KV_LORA
    for p in range(2):
        raw = proj[:, o + p * LANE:o + (p + 1) * LANE]
        rot = proj[:, o + (2 + p) * LANE:o + (3 + p) * LANE]
        kr_ref[0, :, p * LANE:(p + 1) * LANE] = (raw * cos + rot * sin).astype(BF16)
    q = _dot(cq, wuq_ref[...])
    hn_w = MLA_HEADS * QK_NOPE
    hr_w = MLA_HEADS * QK_ROPE
    qn_ref[0] = (q[:, :hn_w] * scale).astype(BF16)
    for p in range(hr_w // LANE):
        raw = q[:, hn_w + p * LANE:hn_w + (p + 1) * LANE]
        rot = q[:, hn_w + hr_w + p * LANE:hn_w + hr_w + (p + 1) * LANE]
        qr_ref[0, :, p * LANE:(p + 1) * LANE] = ((raw * cos + rot * sin) * scale).astype(BF16)
    kn_ref[0] = _dot(ckv, wuk_ref[...]).astype(BF16)
    vt_ref[...] = _dot_nt(wvt_ref[...], ckv).astype(BF16)


def _mla_proj(h, g, win, qg, kvg, wuq, wuk, wvt, cos, sin):
    batch = h.shape[0]
    tm = SEQ_TILE
    n_t = TP // tm
    hn_w = MLA_HEADS * QK_NOPE
    hr_w = MLA_HEADS * QK_ROPE
    seq_blk = lambda w: pl.BlockSpec((1, tm, w), lambda b, t: (b, t, 0))
    consts = (g, win, qg, kvg, wuq, wuk, wvt)
    out_w = (hn_w, hr_w, hn_w, 2 * LANE)
    return pl.pallas_call(
        _mla_proj_kernel,
        grid=(batch, n_t),
        in_specs=[seq_blk(D_MODEL)] + [_const_spec(c.shape) for c in consts]
        + [pl.BlockSpec((tm, LANE), lambda b, t: (t, 0))] * 2,
        out_specs=[seq_blk(w) for w in out_w]
        + [pl.BlockSpec((hn_w, tm), lambda b, t: (0, b * n_t + t))],
        out_shape=[jax.ShapeDtypeStruct((batch, TP, w), BF16) for w in out_w]
        + [jax.ShapeDtypeStruct((hn_w, batch * TP), BF16)],
        compiler_params=_params("parallel", "parallel"),
        name="mla_proj",
    )(h, *consts, cos, sin)


def _attn_kernel(qn_ref, qr_ref, qnm_ref, qrm_ref, kn_ref, kr_ref, vt_ref, o_ref,
                 s_sc, smax_sc, m_sc, acc_sc):
    i = pl.program_id(2)
    t = ATT_TILE
    base = PAD + N_META
    heads = range(HEAD_GROUP)
    h_cols = lambda hh: slice(hh * LANE, (hh + 1) * LANE)
    r_cols = lambda hh: slice((hh // 2) * LANE, (hh // 2 + 1) * LANE)
    chains = []
    for hh in heads:
        q = jnp.concatenate([qn_ref[0, :, h_cols(hh)], qr_ref[0, :, r_cols(hh)]], axis=1)
        for c in range(t // MXU_DIM):
            chains.append((hh, c, q[c * MXU_DIM:(c + 1) * MXU_DIM, :]))

    def keys(hh, start, size):
        half = slice((hh % 2) * LANE, (hh % 2 + 1) * LANE)
        return jnp.concatenate([kn_ref[0, pl.ds(start, size), h_cols(hh)],
                                kr_ref[0, pl.ds(start, size), half]], axis=1)

    def values_t(hh, start, size):
        return jnp.concatenate([vt_ref[hh * V_HEAD:(hh + 1) * V_HEAD, pl.ds(start, size)],
                                jnp.ones((L_ROWS, size), BF16)], axis=0)

    def frame_start(j):
        return pl.multiple_of(base + j * t, LANE)

    def issue_scores(j, buf):
        start = frame_start(j)
        for n, (hh, c, q) in enumerate(chains):
            s = _dot_nt(keys(hh, start, t), q)
            s_sc[buf, n] = s
            smax_sc[buf, n] = jnp.max(s, axis=0, keepdims=True)

    def softmax_pv(n, s, v_t, first=False, s_max=None):
        m_new = jnp.max(s, axis=0, keepdims=True) if s_max is None else s_max
        if first:
            p = jnp.exp2(s - m_new)
            acc_sc[n] = _dot(v_t, p.astype(BF16))
        else:
            m_prev = m_sc[n]
            m_new = jnp.maximum(m_prev, m_new)
            alpha = jnp.exp2(m_prev - m_new)
            p = jnp.exp2(s - m_new)
            acc_sc[n] = alpha * acc_sc[n] + _dot(v_t, p.astype(BF16))
        m_sc[n] = m_new

    def consume(j, buf, mask=None):
        start = frame_start(j)
        for n, (hh, c, _) in enumerate(chains):
            s = s_sc[buf, n]
            s_max = None
            if mask is not None:
                s = jnp.where(mask(c), s, NEG_BIG)
            else:
                s_max = smax_sc[buf, n]
            softmax_pv(n, s, values_t(hh, start, t), s_max=s_max)

    meta_valid = lax.broadcasted_iota(jnp.int32, (base, 1), 0) >= PAD
    meta_scores = [_dot_nt(keys(hh, 0, base), q) for hh, c, q in chains]
    issue_scores(0, 0)
    for n, (hh, c, _) in enumerate(chains):
        softmax_pv(n, jnp.where(meta_valid, meta_scores[n], NEG_BIG), values_t(hh, 0, base), first=True)

    def pair(jj, carry):
        j = 2 * jj
        issue_scores(j + 1, 1)
        consume(j, 0)
        issue_scores(j + 2, 0)
        consume(j + 1, 1)
        return carry

    lax.fori_loop(0, i // 2, pair, 0)

    def diag_mask(c):
        kk = lax.broadcasted_iota(jnp.int32, (t, 1), 0)
        qq = c * MXU_DIM + lax.broadcasted_iota(jnp.int32, (1, MXU_DIM), 1)
        return kk < (qq // CHUNK + 1) * CHUNK

    @pl.when(i % 2 == 1)
    def _():
        issue_scores(i, 1)
        consume(i - 1, 0)
        consume(i, 1, diag_mask)

    @pl.when(i % 2 == 0)
    def _():
        consume(i, 0, diag_mask)

    row0 = frame_start(i)
    for n, (hh, c, _) in enumerate(chains):
        o_t = acc_sc[n, 0:V_HEAD] / acc_sc[n, V_HEAD:V_HEAD + 1]
        rows = pl.ds(pl.multiple_of(row0 + c * MXU_DIM, LANE), MXU_DIM)
        o_ref[0, rows, h_cols(hh)] = o_t.T.astype(BF16)

    @pl.when(i == 0)
    def _():
        for hh in heads:
            qm = jnp.concatenate([qnm_ref[0, :, h_cols(hh)], qrm_ref[0, :, r_cols(hh)]], axis=1)
            s = jnp.where(meta_valid, _dot_nt(keys(hh, 0, base), qm), NEG_BIG)
            p = jnp.exp2(s - jnp.max(s, axis=0, keepdims=True))
            pv = _dot(values_t(hh, 0, base), p.astype(BF16))
            o_t = pv[0:V_HEAD] / pv[V_HEAD:V_HEAD + 1]
            o_ref[0, 0:base, h_cols(hh)] = o_t.T.astype(BF16)


def _attention(qn, qr, kn, kr, vt):
    batch = qn.shape[0]
    t = ATT_TILE
    base = PAD + N_META
    gw = HEAD_GROUP * LANE
    n_chain = HEAD_GROUP * (t // MXU_DIM)

    def frames(width):
        shape = (pl.Element(1), pl.Element(t), pl.Element(width))
        return pl.BlockSpec(shape, lambda b, g, i: (b, pl.multiple_of(base + i * t, LANE),
                                                    pl.multiple_of(g * width, LANE)))

    return pl.pallas_call(
        _attn_kernel,
        grid=(batch, MLA_HEADS // HEAD_GROUP, SEQ // t),
        in_specs=[frames(gw), frames(gw // 2),
                  pl.BlockSpec((1, base, gw), lambda b, g, i: (b, 0, g)),
                  pl.BlockSpec((1, base, gw // 2), lambda b, g, i: (b, 0, g)),
                  pl.BlockSpec((1, TP, gw), lambda b, g, i: (b, 0, g)),
                  pl.BlockSpec((1, TP, 2 * LANE), lambda b, g, i: (b, 0, 0)),
                  pl.BlockSpec((gw, TP), lambda b, g, i: (g, b))],
        out_specs=pl.BlockSpec((1, TP, gw), lambda b, g, i: (b, 0, g)),
        out_shape=jax.ShapeDtypeStruct((batch, TP, MLA_HEADS * V_HEAD), BF16),
        scratch_shapes=[pltpu.VMEM((2, n_chain, t, MXU_DIM), F32),
                        pltpu.VMEM((2, n_chain, 1, MXU_DIM), F32),
                        pltpu.VMEM((n_chain, 1, MXU_DIM), F32),
                        pltpu.VMEM((n_chain, V_HEAD + L_ROWS, MXU_DIM), F32)],
        compiler_params=_params("parallel", "parallel", "arbitrary"),
        name="mla_attention",
    )(qn, qr, qn, qr, kn, kr, vt)


def _proj_residual_kernel(h_ref, a_ref, w_ref, o_ref):
    o_ref[...] = h_ref[...] + _dot(a_ref[...], w_ref[...])


def _proj_residual(h, a, w, layer):
    batch = h.shape[0]
    rows = batch * TP
    tm = PROJ_ROW_TILE
    blk = lambda width: pl.BlockSpec((tm, width), lambda i: (i, 0))
    return pl.pallas_call(
        _proj_residual_kernel,
        grid=(rows // tm,),
        in_specs=[blk(D_MODEL), blk(a.shape[2]), _layer_spec(w.shape[1:], layer)],
        out_specs=blk(D_MODEL),
        out_shape=jax.ShapeDtypeStruct((rows, D_MODEL), F32),
        compiler_params=_params("parallel"),
        name="proj_residual",
    )(h.reshape(rows, D_MODEL), a.reshape(rows, a.shape[2]), w).reshape(h.shape)


def _ffn_kernel(h_ref, g_ref, wg_ref, wu_ref, wd_ref, o_ref, hn_sc):
    @pl.when(pl.program_id(1) == 0)
    def _():
        x = h_ref[...]
        hn_sc[...] = _rms(x, g_ref[...]).astype(BF16)
        o_ref[...] = x

    hn = hn_sc[...]
    gate = _dot(hn, wg_ref[...])
    up = _dot(hn, wu_ref[...])
    act = (gate * jax.nn.sigmoid(gate) * up).astype(BF16)
    o_ref[...] += _dot(act, wd_ref[...])


def _ffn(h, g, w_gu, w_down, layer):
    rows = h.shape[0]
    tm = FFN_ROW_TILE
    n_f = D_FF // FF_TILE
    return pl.pallas_call(
        _ffn_kernel,
        grid=(rows // tm, n_f),
        in_specs=[pl.BlockSpec((tm, D_MODEL), lambda i, j: (i, 0)),
                  pl.BlockSpec((1, D_MODEL), lambda i, j: (0, 0)),
                  pl.BlockSpec((None, D_MODEL, FF_TILE), lambda i, j: (layer, 0, j)),
                  pl.BlockSpec((None, D_MODEL, FF_TILE), lambda i, j: (layer, 0, j + n_f)),
                  pl.BlockSpec((None, FF_TILE, D_MODEL), lambda i, j: (layer, j, 0))],
        out_specs=pl.BlockSpec((tm, D_MODEL), lambda i, j: (i, 0)),
        out_shape=jax.ShapeDtypeStruct(h.shape, F32),
        scratch_shapes=[pltpu.VMEM((tm, D_MODEL), BF16)],
        compiler_params=_params("parallel", "arbitrary"),
        name="ffn",
    )(h, g, w_gu, w_gu, w_down)


def _lru_in_kernel(h_ref, g_ref, w_ref, xb_ref, yg_ref):
    half = LRU_IN_TILE // 2
    for r in range(2):
        rows = slice(r * half, (r + 1) * half)
        hn = _rms(h_ref[rows, :], g_ref[...]).astype(BF16)
        xb_ref[rows, :] = _dot(hn, w_ref[:, :D_RNN])
        y = _dot(hn, w_ref[:, D_RNN:])
        c = (2.0 / jnp.pi) ** 0.5
        yg_ref[rows, :] = 0.5 * y * (1.0 + jnp.tanh(c * (y + 0.044715 * (y * y * y))))


def _lru_in(h, g, w, layer):
    rows = h.shape[0]
    tm = LRU_IN_TILE
    row = lambda width: pl.BlockSpec((tm, width), lambda i: (i, 0))
    return pl.pallas_call(
        _lru_in_kernel,
        grid=(rows // tm,),
        in_specs=[row(D_MODEL), _const_spec(g.shape), _layer_spec(w.shape[1:], layer)],
        out_specs=[row(D_RNN), row(D_RNN)],
        out_shape=[jax.ShapeDtypeStruct((rows, D_RNN), F32)] * 2,
        compiler_params=_params("parallel"),
        name="lru_in",
    )(h, g, w)


def _lru_kernel(xb_ref, yg_ref, cw_ref, cb_ref, wga_ref, bga_ref, wgx_ref, bgx_ref, lam_ref,
                o_ref, x_sc, tail_sc, a_sc, b_sc, hs_sc, h_sc):
    t = pl.program_id(1)
    tt = SEQ_TILE
    halo = 8

    x_sc[...] = xb_ref[0]

    @pl.when(t == 0)
    def _():
        x_sc[0:PAD, :] = jnp.zeros((PAD, D_RNN), F32)
        tail_sc[...] = jnp.zeros(tail_sc.shape, F32)
        h_sc[...] = jnp.zeros(h_sc.shape, F32)

    neg_lam = -lam_ref[...]
    softplus = jnp.maximum(neg_lam, 0.0) + jnp.log1p(jnp.exp(-jnp.abs(neg_lam)))
    neg_log_a_rate = LRU_C * softplus
    a_exp2_rate = -math.log2(math.e) * neg_log_a_rate
    for g in range(D_RNN // GATE_TILE):
        cols = slice(g * GATE_TILE, (g + 1) * GATE_TILE)
        x = x_sc[:, cols]
        taps = [cw_ref[k:k + 1, cols] for k in range(CONV_W)]
        xc = cb_ref[:, cols] + taps[CONV_W - 1] * x
        for d in range(1, CONV_W):
            xc = xc + taps[CONV_W - 1 - d] * pltpu.roll(x, d, axis=0)
        head = jnp.concatenate([tail_sc[:, cols], x[0:halo]], axis=0)
        xc_head = cb_ref[:, cols]
        for k in range(CONV_W):
            o = halo - (CONV_W - 1) + k
            xc_head = xc_head + taps[k] * head[o:o + halo]
        xc = jnp.concatenate([xc_head, xc[halo:]], axis=0)
        xc16 = xc.astype(BF16)
        r = jax.nn.sigmoid(_dot(xc16, wga_ref[g]) + bga_ref[:, cols])
        ig = jax.nn.sigmoid(_dot(xc16, wgx_ref[g]) + bgx_ref[:, cols])
        a = jnp.exp2(r * a_exp2_rate[:, cols])
        a_sc[:, :, cols] = a.reshape(tt // SCAN_BLOCK, SCAN_BLOCK, GATE_TILE)
        gain = jnp.sqrt(jnp.tanh(r * neg_log_a_rate[:, cols]) * (1.0 + a * a))
        b_sc[:, :, cols] = (gain * (ig * xc)).reshape(tt // SCAN_BLOCK, SCAN_BLOCK, GATE_TILE)

    tail_sc[...] = x_sc[tt - halo:tt, :]

    @pl.when(t == 0)
    def _():
        b_sc[0:PAD // SCAN_BLOCK] = jnp.zeros((PAD // SCAN_BLOCK, SCAN_BLOCK, D_RNN), F32)

    def body(blk, h):
        a_cum = b_cum = h_out = None
        for k in range(SCAN_BLOCK):
            a_k = a_sc[blk, k:k + 1, :]
            b_k = b_sc[blk, k:k + 1, :]
            if k == 0:
                a_cum, b_cum = a_k, b_k
            else:
                a_cum, b_cum = a_k * a_cum, a_k * b_cum + b_k
            h_out = a_cum * h + b_cum
            hs_sc[blk, k:k + 1, :] = h_out
        return h_out

    h_sc[...] = lax.fori_loop(0, tt // SCAN_BLOCK, body, h_sc[...], unroll=4)
    o_ref[0] = (hs_sc[...].reshape(tt, D_RNN) * yg_ref[0]).astype(BF16)


def _lru(xb, yg, cw, cb, wga, bga, wgx, bgx, lam):
    batch = xb.shape[0]
    tt = SEQ_TILE
    blk = pl.BlockSpec((1, tt, D_RNN), lambda b, t: (b, t, 0))
    consts = (cw, cb, wga, bga, wgx, bgx, lam)
    return pl.pallas_call(
        _lru_kernel,
        grid=(batch, TP // tt),
        in_specs=[blk, blk] + [_const_spec(c.shape) for c in consts],
        out_specs=blk,
        out_shape=jax.ShapeDtypeStruct(xb.shape, BF16),
        scratch_shapes=[pltpu.VMEM((tt, D_RNN), F32), pltpu.VMEM((8, D_RNN), F32)]
        + [pltpu.VMEM((tt // SCAN_BLOCK, SCAN_BLOCK, D_RNN), F32)] * 3
        + [pltpu.VMEM((1, D_RNN), F32)],
        compiler_params=_params("parallel", "arbitrary"),
        name="rglru",
    )(xb, yg, *consts)


def _final_norm_kernel(h_ref, g_ref, o_ref):
    o_ref[0] = _rms(h_ref[0], g_ref[...])


def _final_norm(h, g):
    batch = h.shape[0]
    return pl.pallas_call(
        _final_norm_kernel,
        grid=(batch, SEQ // ROW_TILE),
        in_specs=[pl.BlockSpec((pl.Element(1), pl.Element(ROW_TILE), pl.Element(D_MODEL)),
                               lambda b, i: (b, pl.multiple_of(PAD + N_META + i * ROW_TILE, LANE), 0)),
                  _const_spec(g.shape)],
        out_specs=pl.BlockSpec((1, ROW_TILE, D_MODEL), lambda b, i: (b, i, 0)),
        out_shape=jax.ShapeDtypeStruct((batch, SEQ, D_MODEL), F32),
        compiler_params=_params("parallel", "parallel"),
        name="final_norm",
    )(h, g)


def _rot_half_cols(w):
    return jnp.roll(w, -QK_ROPE // 2, axis=-1)


def _mla_weights(w_in, w_uq, w_ukv):
    w_q = w_in[:, :Q_LORA]
    w_kv = w_in[:, Q_LORA:Q_LORA + KV_LORA]
    w_kr = w_in[:, Q_LORA + KV_LORA:]
    z = jnp.zeros_like(w_kr)
    pair = lambda w: [w, z, z, w]
    win = jnp.concatenate([w_q, w_kv] + pair(w_kr) + pair(_rot_half_cols(w_kr)), axis=1)
    uq = w_uq.reshape(Q_LORA, MLA_HEADS, QK_NOPE + QK_ROPE)
    uq_n = uq[:, :, :QK_NOPE].reshape(Q_LORA, -1)
    uq_r = uq[:, :, QK_NOPE:]
    wuq = jnp.concatenate([uq_n, uq_r.reshape(Q_LORA, -1),
                           _rot_half_cols(uq_r).reshape(Q_LORA, -1)], axis=1)
    ukv = w_ukv.reshape(KV_LORA, MLA_HEADS, QK_NOPE + V_HEAD)
    wuk = ukv[:, :, :QK_NOPE].reshape(KV_LORA, -1)
    wvt = ukv[:, :, QK_NOPE:].reshape(KV_LORA, -1).T
    return win.astype(BF16), wuq.astype(BF16), wuk.astype(BF16), wvt.astype(BF16)


def _gate_block_diag(w):
    w = w.reshape(RNN_BLOCKS // 2, 2, RNN_BW, RNN_BW)
    z = jnp.zeros_like(w[:, 0])
    top = jnp.concatenate([w[:, 0], z], axis=2)
    bot = jnp.concatenate([z, w[:, 1]], axis=2)
    return jnp.concatenate([top, bot], axis=1).astype(BF16)


def _rope_tables():
    pos = jnp.maximum(jnp.arange(TP) - PAD, 0).astype(F32)
    inv_freq = ROPE_THETA ** (-jnp.arange(0, QK_ROPE, 2, dtype=F32) / QK_ROPE)
    ang = pos[:, None] * inv_freq[None, :]
    cos, sin = jnp.cos(ang), jnp.sin(ang)
    cos = jnp.concatenate([cos, cos, cos, cos], axis=1)
    sin = jnp.concatenate([-sin, sin, -sin, sin], axis=1)
    return cos, sin


def _mla_layer(h, g_mix, w_in, qg, kvg, w_uq, w_ukv, w_o_all, layer, cos, sin):
    win, wuq, wuk, wvt = _mla_weights(w_in, w_uq, w_ukv)
    qn, qr, kn, kr, vt = _mla_proj(h, g_mix, win, qg, kvg, wuq, wuk, wvt, cos, sin)
    return _proj_residual(h, _attention(qn, qr, kn, kr, vt), w_o_all, layer)


def _lru_layer(h, g_mix, w_in_all, cw, cb, w_ga, b_ga, w_gx, b_gx, lam, w_o_all, layer):
    batch = h.shape[0]
    hf = h.reshape(batch * TP, D_MODEL)
    xb, yg = (a.reshape(batch, TP, D_RNN) for a in _lru_in(hf, g_mix, w_in_all, layer))
    mixed = _lru(xb, yg, cw, cb, _gate_block_diag(w_ga), b_ga, _gate_block_diag(w_gx), b_gx, lam)
    return _proj_residual(h, mixed, w_o_all, layer)


def kernel(x, meta_tokens, norm_mix, norm_ffn, norm_final, mla_w_in, mla_q_norm, mla_kv_norm, mla_w_uq, mla_w_ukv, mla_w_o, lru_w_in, lru_conv_w, lru_conv_b, lru_w_gate_a, lru_b_gate_a, lru_w_gate_x, lru_b_gate_x, lru_lambda, lru_w_o, ffn_w_gu, ffn_w_down):
    batch = x.shape[0]
    rows = batch * TP
    meta = jnp.broadcast_to(meta_tokens.astype(x.dtype)[None], (batch, N_META, D_MODEL))
    h = jnp.concatenate([jnp.zeros((batch, PAD, D_MODEL), x.dtype), meta, x], axis=1)
    cos, sin = _rope_tables()
    row_vec = lambda v: v.reshape(1, -1)
    flat = lambda a: a.reshape(rows, a.shape[-1])
    seq = lambda a: a.reshape(batch, TP, a.shape[-1])
    mla_w_o, lru_w_in, lru_w_o, ffn_w_gu, ffn_w_down = (
        w.astype(BF16) for w in (mla_w_o, lru_w_in, lru_w_o, ffn_w_gu, ffn_w_down))

    for layer in range(DEPTH):
        j = layer // N_MIXERS
        g_mix = row_vec(norm_mix[layer])
        if layer % N_MIXERS == 0:
            h = _mla_layer(h, g_mix, mla_w_in[j], row_vec(mla_q_norm[j]), row_vec(mla_kv_norm[j]),
                           mla_w_uq[j], mla_w_ukv[j], mla_w_o, j, cos, sin)
        else:
            h = _lru_layer(h, g_mix, lru_w_in, lru_conv_w[j], row_vec(lru_conv_b[j]),
                           lru_w_gate_a[j], row_vec(lru_b_gate_a[j]), lru_w_gate_x[j],
                           row_vec(lru_b_gate_x[j]), row_vec(lru_lambda[j]), lru_w_o, j)
        h = seq(_ffn(flat(h), row_vec(norm_ffn[layer]), ffn_w_gu, ffn_w_down, layer))
    return _final_norm(h, row_vec(norm_final))
```

```python
import functools
import math

import jax
import jax.numpy as jnp
from jax import lax
from jax.experimental import pallas as pl
from jax.experimental.pallas import tpu as pltpu

D_MODEL = 2048
SEQ = 4096
CHUNK = 64
N_META = 16
DEPTH = 4
N_MIXERS = 2

MLA_HEADS = 16
Q_LORA = 512
KV_LORA = 512
QK_NOPE = 128
QK_ROPE = 64
V_HEAD = 128
ROPE_THETA = 10000.0

D_RNN = D_MODEL
RNN_BLOCKS = 16
RNN_BW = D_RNN // RNN_BLOCKS
CONV_W = 4
LRU_C = 8.0

D_FF = -(-8 * D_MODEL // (3 * 256)) * 256

RMS_EPS = 1e-6
NEG_BIG = -1e30

LANE = 128
MXU_DIM = 256
PAD = LANE - N_META
TP = PAD + N_META + SEQ
VMEM_LIMIT = 56 * 1024 * 1024

ROW_TILE = 512
SEQ_TILE = 384
LRU_IN_TILE = 512
PROJ_ROW_TILE = 768
FFN_ROW_TILE = 1056
FF_TILE = 512
ATT_TILE = 512
HEAD_GROUP = 4
L_ROWS = 16
GATE_TILE = 2 * RNN_BW
SCAN_BLOCK = 8

BF16 = jnp.bfloat16
F32 = jnp.float32


def _params(*sem):
    return pltpu.CompilerParams(dimension_semantics=sem, vmem_limit_bytes=VMEM_LIMIT)


def _rms(x, g):
    ms = jnp.mean(x * x, axis=-1, keepdims=True)
    return x * lax.rsqrt(ms + RMS_EPS) * g


def _dot(a, b):
    return jnp.dot(a, b, preferred_element_type=F32)


def _dot_nt(a, b):
    return lax.dot_general(a, b, (((1,), (1,)), ((), ())), preferred_element_type=F32)


def _const_spec(shape):
    return pl.BlockSpec(shape, lambda *_: (0,) * len(shape))


def _layer_spec(shape, layer):
    return pl.BlockSpec((None,) + tuple(shape), lambda *_: (layer,) + (0,) * len(shape))


def _mla_proj_kernel(*refs, from_frames):
    if from_frames:
        (x_ref, meta_ref, g_ref, win_ref, qg_ref, kvg_ref, wuq_ref, wuk_ref, wvt_ref,
         cos_ref, sin_ref, qn_ref, qr_ref, kn_ref, kr_ref, vt_ref, h0_ref) = refs
        frames = x_ref[0]
        first = jnp.concatenate([jnp.zeros((PAD, D_MODEL), F32), meta_ref[...],
                                 frames[:SEQ_TILE - PAD - N_META]], axis=0)
        h = jnp.where(pl.program_id(1) == 0, first, frames)
        h0_ref[0] = h
    else:
        (h_ref, g_ref, win_ref, qg_ref, kvg_ref, wuq_ref, wuk_ref, wvt_ref,
         cos_ref, sin_ref, qn_ref, qr_ref, kn_ref, kr_ref, vt_ref) = refs
        h = h_ref[0]
    scale = (QK_NOPE + QK_ROPE) ** -0.5 * math.log2(math.e)
    hn = _rms(h, g_ref[...]).astype(BF16)
    proj = _dot(hn, win_ref[...])
    cq = _rms(proj[:, :Q_LORA], qg_ref[...]).astype(BF16)
    ckv = _rms(proj[:, Q_LORA:Q_LORA + KV_LORA], kvg_ref[...]).astype(BF16)
    cos = cos_ref[...]
    sin = sin_ref[...]
    o = Q_LORA + KV_LORA
    for p in range(2):
        raw = proj[:, o + p * LANE:o + (p + 1) * LANE]
        rot = proj[:, o + (2 + p) * LANE:o + (3 + p) * LANE]
        kr_ref[0, :, p * LANE:(p + 1) * LANE] = (raw * cos + rot * sin).astype(BF16)
    q = _dot(cq, wuq_ref[...])
    hn_w = MLA_HEADS * QK_NOPE
    hr_w = MLA_HEADS * QK_ROPE
    qn_ref[0] = (q[:, :hn_w] * scale).astype(BF16)
    for p in range(hr_w // LANE):
        raw = q[:, hn_w + p * LANE:hn_w + (p + 1) * LANE]
        rot = q[:, hn_w + hr_w + p * LANE:hn_w + hr_w + (p + 1) * LANE]
        qr_ref[0, :, p * LANE:(p + 1) * LANE] = ((raw * cos + rot * sin) * scale).astype(BF16)
    kn_ref[0] = _dot(ckv, wuk_ref[...]).astype(BF16)
    vt_ref[...] = _dot_nt(wvt_ref[...], ckv).astype(BF16)


def _mla_proj(h, g, win, qg, kvg, wuq, wuk, wvt, cos, sin, meta=None):
    batch = h.shape[0]
    tm = SEQ_TILE
    n_t = TP // tm
    hn_w = MLA_HEADS * QK_NOPE
    hr_w = MLA_HEADS * QK_ROPE
    seq_blk = lambda w: pl.BlockSpec((1, tm, w), lambda b, t: (b, t, 0))
    consts = (g, win, qg, kvg, wuq, wuk, wvt)
    out_w = (hn_w, hr_w, hn_w, 2 * LANE)
    out_specs = ([seq_blk(w) for w in out_w]
                 + [pl.BlockSpec((hn_w, tm), lambda b, t: (0, b * n_t + t))])
    out_shape = ([jax.ShapeDtypeStruct((batch, TP, w), BF16) for w in out_w]
                 + [jax.ShapeDtypeStruct((hn_w, batch * TP), BF16)])
    if meta is None:
        tokens, token_specs = (h,), [seq_blk(D_MODEL)]
    else:
        first_frame = lambda t: pl.multiple_of(jnp.maximum(t * tm - (PAD + N_META), 0), LANE)
        frame_spec = pl.BlockSpec((pl.Element(1), pl.Element(tm), pl.Element(D_MODEL)),
                                  lambda b, t: (b, first_frame(t), 0))
        tokens, token_specs = (h, meta), [frame_spec, _const_spec(meta.shape)]
        out_specs.append(seq_blk(D_MODEL))
        out_shape.append(jax.ShapeDtypeStruct((batch, TP, D_MODEL), F32))
    return pl.pallas_call(
        functools.partial(_mla_proj_kernel, from_frames=meta is not None),
        grid=(batch, n_t),
        in_specs=token_specs + [_const_spec(c.shape) for c in consts]
        + [pl.BlockSpec((tm, LANE), lambda b, t: (t, 0))] * 2,
        out_specs=out_specs,
        out_shape=out_shape,
        compiler_params=_params("parallel", "parallel"),
        name="mla_proj",
    )(*tokens, *consts, cos, sin)


def _attn_kernel(qn_ref, qr_ref, qnm_ref, qrm_ref, kn_ref, kr_ref, vt_ref, o_ref,
                 s_sc, smax_sc, m_sc, acc_sc):
    i = pl.program_id(2)
    t = ATT_TILE
    base = PAD + N_META
    heads = range(HEAD_GROUP)
    h_cols = lambda hh: slice(hh * LANE, (hh + 1) * LANE)
    r_cols = lambda hh: slice((hh // 2) * LANE, (hh // 2 + 1) * LANE)
    chains = []
    for hh in heads:
        q = jnp.concatenate([qn_ref[0, :, h_cols(hh)], qr_ref[0, :, r_cols(hh)]], axis=1)
        for c in range(t // MXU_DIM):
            chains.append((hh, c, q[c * MXU_DIM:(c + 1) * MXU_DIM, :]))

    def keys(hh, start, size):
        half = slice((hh % 2) * LANE, (hh % 2 + 1) * LANE)
        return jnp.concatenate([kn_ref[0, pl.ds(start, size), h_cols(hh)],
                                kr_ref[0, pl.ds(start, size), half]], axis=1)

    def values_t(hh, start, size):
        return jnp.concatenate([vt_ref[hh * V_HEAD:(hh + 1) * V_HEAD, pl.ds(start, size)],
                                jnp.ones((L_ROWS, size), BF16)], axis=0)

    def frame_start(j):
        return pl.multiple_of(base + j * t, LANE)

    def issue_scores(j, buf):
        start = frame_start(j)
        for n, (hh, c, q) in enumerate(chains):
            s = _dot_nt(keys(hh, start, t), q)
            s_sc[buf, n] = s
            smax_sc[buf, n] = jnp.max(s, axis=0, keepdims=True)

    def softmax_pv(n, s, v_t, first=False, s_max=None):
        m_new = jnp.max(s, axis=0, keepdims=True) if s_max is None else s_max
        if first:
            p = jnp.exp2(s - m_new)
            acc_sc[n] = _dot(v_t, p.astype(BF16))
        else:
            m_prev = m_sc[n]
            m_new = jnp.maximum(m_prev, m_new)
            alpha = jnp.exp2(m_prev - m_new)
            p = jnp.exp2(s - m_new)
            acc_sc[n] = alpha * acc_sc[n] + _dot(v_t, p.astype(BF16))
        m_sc[n] = m_new

    def consume(j, buf, mask=None):
        start = frame_start(j)
        for n, (hh, c, _) in enumerate(chains):
            s = s_sc[buf, n]
            s_max = None
            if mask is not None:
                s = jnp.where(mask(c), s, NEG_BIG)
            else:
                s_max = smax_sc[buf, n]
            softmax_pv(n, s, values_t(hh, start, t), s_max=s_max)

    meta_valid = lax.broadcasted_iota(jnp.int32, (base, 1), 0) >= PAD
    meta_scores = [_dot_nt(keys(hh, 0, base), q) for hh, c, q in chains]
    issue_scores(0, 0)
    for n, (hh, c, _) in enumerate(chains):
        softmax_pv(n, jnp.where(meta_valid, meta_scores[n], NEG_BIG), values_t(hh, 0, base), first=True)

    def pair(jj, carry):
        j = 2 * jj
        issue_scores(j + 1, 1)
        consume(j, 0)
        issue_scores(j + 2, 0)
        consume(j + 1, 1)
        return carry

    lax.fori_loop(0, i // 2, pair, 0)

    def diag_mask(c):
        kk = lax.broadcasted_iota(jnp.int32, (t, 1), 0)
        qq = c * MXU_DIM + lax.broadcasted_iota(jnp.int32, (1, MXU_DIM), 1)
        return kk < (qq // CHUNK + 1) * CHUNK

    @pl.when(i % 2 == 1)
    def _():
        issue_scores(i, 1)
        consume(i - 1, 0)
        consume(i, 1, diag_mask)

    @pl.when(i % 2 == 0)
    def _():
        consume(i, 0, diag_mask)

    row0 = frame_start(i)
    for n, (hh, c, _) in enumerate(chains):
        o_t = acc_sc[n, 0:V_HEAD] / acc_sc[n, V_HEAD:V_HEAD + 1]
        rows = pl.ds(pl.multiple_of(row0 + c * MXU_DIM, LANE), MXU_DIM)
        o_ref[0, rows, h_cols(hh)] = o_t.T.astype(BF16)

    @pl.when(i == 0)
    def _():
        for hh in heads:
            qm = jnp.concatenate([qnm_ref[0, :, h_cols(hh)], qrm_ref[0, :, r_cols(hh)]], axis=1)
            s = jnp.where(meta_valid, _dot_nt(keys(hh, 0, base), qm), NEG_BIG)
            p = jnp.exp2(s - jnp.max(s, axis=0, keepdims=True))
            pv = _dot(values_t(hh, 0, base), p.astype(BF16))
            o_t = pv[0:V_HEAD] / pv[V_HEAD:V_HEAD + 1]
            o_ref[0, 0:base, h_cols(hh)] = o_t.T.astype(BF16)


def _attention(qn, qr, kn, kr, vt):
    batch = qn.shape[0]
    t = ATT_TILE
    base = PAD + N_META
    gw = HEAD_GROUP * LANE
    n_chain = HEAD_GROUP * (t // MXU_DIM)

    def frames(width):
        shape = (pl.Element(1), pl.Element(t), pl.Element(width))
        return pl.BlockSpec(shape, lambda b, g, i: (b, pl.multiple_of(base + i * t, LANE),
                                                    pl.multiple_of(g * width, LANE)))

    return pl.pallas_call(
        _attn_kernel,
        grid=(batch, MLA_HEADS // HEAD_GROUP, SEQ // t),
        in_specs=[frames(gw), frames(gw // 2),
                  pl.BlockSpec((1, base, gw), lambda b, g, i: (b, 0, g)),
                  pl.BlockSpec((1, base, gw // 2), lambda b, g, i: (b, 0, g)),
                  pl.BlockSpec((1, TP, gw), lambda b, g, i: (b, 0, g)),
                  pl.BlockSpec((1, TP, 2 * LANE), lambda b, g, i: (b, 0, 0)),
                  pl.BlockSpec((gw, TP), lambda b, g, i: (g, b))],
        out_specs=pl.BlockSpec((1, TP, gw), lambda b, g, i: (b, 0, g)),
        out_shape=jax.ShapeDtypeStruct((batch, TP, MLA_HEADS * V_HEAD), BF16),
        scratch_shapes=[pltpu.VMEM((2, n_chain, t, MXU_DIM), F32),
                        pltpu.VMEM((2, n_chain, 1, MXU_DIM), F32),
                        pltpu.VMEM((n_chain, 1, MXU_DIM), F32),
                        pltpu.VMEM((n_chain, V_HEAD + L_ROWS, MXU_DIM), F32)],
        compiler_params=_params("parallel", "parallel", "arbitrary"),
        name="mla_attention",
    )(qn, qr, qn, qr, kn, kr, vt)


def _proj_residual_kernel(h_ref, a_ref, w_ref, o_ref):
    o_ref[...] = h_ref[...] + _dot(a_ref[...], w_ref[...])


def _proj_residual(h, a, w, layer):
    batch = h.shape[0]
    rows = batch * TP
    tm = PROJ_ROW_TILE
    blk = lambda width: pl.BlockSpec((tm, width), lambda i: (i, 0))
    return pl.pallas_call(
        _proj_residual_kernel,
        grid=(rows // tm,),
        in_specs=[blk(D_MODEL), blk(a.shape[2]), _layer_spec(w.shape[1:], layer)],
        out_specs=blk(D_MODEL),
        out_shape=jax.ShapeDtypeStruct((rows, D_MODEL), F32),
        compiler_params=_params("parallel"),
        name="proj_residual",
    )(h.reshape(rows, D_MODEL), a.reshape(rows, a.shape[2]), w).reshape(h.shape)


def _ffn_kernel(h_ref, g_ref, wg_ref, wu_ref, wd_ref, o_ref, hn_sc):
    @pl.when(pl.program_id(1) == 0)
    def _():
        x = h_ref[...]
        hn_sc[...] = _rms(x, g_ref[...]).astype(BF16)
        o_ref[...] = x

    hn = hn_sc[...]
    gate = _dot(hn, wg_ref[...])
    up = _dot(hn, wu_ref[...])
    act = (gate * jax.nn.sigmoid(gate) * up).astype(BF16)
    o_ref[...] += _dot(act, wd_ref[...])


def _ffn(h, g, w_gu, w_down, layer):
    rows = h.shape[0]
    tm = FFN_ROW_TILE
    n_f = D_FF // FF_TILE
    return pl.pallas_call(
        _ffn_kernel,
        grid=(rows // tm, n_f),
        in_specs=[pl.BlockSpec((tm, D_MODEL), lambda i, j: (i, 0)),
                  pl.BlockSpec((1, D_MODEL), lambda i, j: (0, 0)),
                  pl.BlockSpec((None, D_MODEL, FF_TILE), lambda i, j: (layer, 0, j)),
                  pl.BlockSpec((None, D_MODEL, FF_TILE), lambda i, j: (layer, 0, j + n_f)),
                  pl.BlockSpec((None, FF_TILE, D_MODEL), lambda i, j: (layer, j, 0))],
        out_specs=pl.BlockSpec((tm, D_MODEL), lambda i, j: (i, 0)),
        out_shape=jax.ShapeDtypeStruct(h.shape, F32),
        scratch_shapes=[pltpu.VMEM((tm, D_MODEL), BF16)],
        compiler_params=_params("parallel", "arbitrary"),
        name="ffn",
    )(h, g, w_gu, w_gu, w_down)


def _lru_in_kernel(h_ref, g_ref, w_ref, xb_ref, yg_ref):
    half = LRU_IN_TILE // 2
    for r in range(2):
        rows = slice(r * half, (r + 1) * half)
        hn = _rms(h_ref[rows, :], g_ref[...]).astype(BF16)
        xb_ref[rows, :] = _dot(hn, w_ref[:, :D_RNN])
        y = _dot(hn, w_ref[:, D_RNN:])
        c = (2.0 / jnp.pi) ** 0.5
        yg_ref[rows, :] = 0.5 * y * (1.0 + jnp.tanh(c * (y + 0.044715 * (y * y * y))))


def _lru_in(h, g, w, layer):
    rows = h.shape[0]
    tm = LRU_IN_TILE
    row = lambda width: pl.BlockSpec((tm, width), lambda i: (i, 0))
    return pl.pallas_call(
        _lru_in_kernel,
        grid=(rows // tm,),
        in_specs=[row(D_MODEL), _const_spec(g.shape), _layer_spec(w.shape[1:], layer)],
        out_specs=[row(D_RNN), row(D_RNN)],
        out_shape=[jax.ShapeDtypeStruct((rows, D_RNN), F32)] * 2,
        compiler_params=_params("parallel"),
        name="lru_in",
    )(h, g, w)


def _lru_kernel(xb_ref, yg_ref, cw_ref, cb_ref, wga_ref, bga_ref, wgx_ref, bgx_ref, lam_ref,
                o_ref, x_sc, tail_sc, a_sc, b_sc, hs_sc, h_sc):
    t = pl.program_id(1)
    tt = SEQ_TILE
    halo = 8

    x_sc[...] = xb_ref[0]

    @pl.when(t == 0)
    def _():
        x_sc[0:PAD, :] = jnp.zeros((PAD, D_RNN), F32)
        tail_sc[...] = jnp.zeros(tail_sc.shape, F32)
        h_sc[...] = jnp.zeros(h_sc.shape, F32)

    neg_lam = -lam_ref[...]
    softplus = jnp.maximum(neg_lam, 0.0) + jnp.log1p(jnp.exp(-jnp.abs(neg_lam)))
    neg_log_a_rate = LRU_C * softplus
    a_exp2_rate = -math.log2(math.e) * neg_log_a_rate
    for g in range(D_RNN // GATE_TILE):
        cols = slice(g * GATE_TILE, (g + 1) * GATE_TILE)
        x = x_sc[:, cols]
        taps = [cw_ref[k:k + 1, cols] for k in range(CONV_W)]
        xc = cb_ref[:, cols] + taps[CONV_W - 1] * x
        for d in range(1, CONV_W):
            xc = xc + taps[CONV_W - 1 - d] * pltpu.roll(x, d, axis=0)
        head = jnp.concatenate([tail_sc[:, cols], x[0:halo]], axis=0)
        xc_head = cb_ref[:, cols]
        for k in range(CONV_W):
            o = halo - (CONV_W - 1) + k
            xc_head = xc_head + taps[k] * head[o:o + halo]
        xc = jnp.concatenate([xc_head, xc[halo:]], axis=0)
        xc16 = xc.astype(BF16)
        r = jax.nn.sigmoid(_dot(xc16, wga_ref[g]) + bga_ref[:, cols])
        ig = jax.nn.sigmoid(_dot(xc16, wgx_ref[g]) + bgx_ref[:, cols])
        a = jnp.exp2(r * a_exp2_rate[:, cols])
        a_sc[:, :, cols] = a.reshape(tt // SCAN_BLOCK, SCAN_BLOCK, GATE_TILE)
        gain = jnp.sqrt(jnp.tanh(r * neg_log_a_rate[:, cols]) * (1.0 + a * a))
        b_sc[:, :, cols] = (gain * (ig * xc)).reshape(tt // SCAN_BLOCK, SCAN_BLOCK, GATE_TILE)

    tail_sc[...] = x_sc[tt - halo:tt, :]

    @pl.when(t == 0)
    def _():
        b_sc[0:PAD // SCAN_BLOCK] = jnp.zeros((PAD // SCAN_BLOCK, SCAN_BLOCK, D_RNN), F32)

    def body(blk, h):
        a_cum = b_cum = h_out = None
        for k in range(SCAN_BLOCK):
            a_k = a_sc[blk, k:k + 1, :]
            b_k = b_sc[blk, k:k + 1, :]
            if k == 0:
                a_cum, b_cum = a_k, b_k
            else:
                a_cum, b_cum = a_k * a_cum, a_k * b_cum + b_k
            h_out = a_cum * h + b_cum
            hs_sc[blk, k:k + 1, :] = h_out
        return h_out

    h_sc[...] = lax.fori_loop(0, tt // SCAN_BLOCK, body, h_sc[...], unroll=4)
    o_ref[0] = (hs_sc[...].reshape(tt, D_RNN) * yg_ref[0]).astype(BF16)


def _lru(xb, yg, cw, cb, wga, bga, wgx, bgx, lam):
    batch = xb.shape[0]
    tt = SEQ_TILE
    blk = pl.BlockSpec((1, tt, D_RNN), lambda b, t: (b, t, 0))
    consts = (cw, cb, wga, bga, wgx, bgx, lam)
    return pl.pallas_call(
        _lru_kernel,
        grid=(batch, TP // tt),
        in_specs=[blk, blk] + [_const_spec(c.shape) for c in consts],
        out_specs=blk,
        out_shape=jax.ShapeDtypeStruct(xb.shape, BF16),
        scratch_shapes=[pltpu.VMEM((tt, D_RNN), F32), pltpu.VMEM((8, D_RNN), F32)]
        + [pltpu.VMEM((tt // SCAN_BLOCK, SCAN_BLOCK, D_RNN), F32)] * 3
        + [pltpu.VMEM((1, D_RNN), F32)],
        compiler_params=_params("parallel", "arbitrary"),
        name="rglru",
    )(xb, yg, *consts)


def _final_norm_kernel(h_ref, g_ref, o_ref):
    o_ref[0] = _rms(h_ref[0], g_ref[...])


def _final_norm(h, g):
    batch = h.shape[0]
    return pl.pallas_call(
        _final_norm_kernel,
        grid=(batch, SEQ // ROW_TILE),
        in_specs=[pl.BlockSpec((pl.Element(1), pl.Element(ROW_TILE), pl.Element(D_MODEL)),
                               lambda b, i: (b, pl.multiple_of(PAD + N_META + i * ROW_TILE, LANE), 0)),
                  _const_spec(g.shape)],
        out_specs=pl.BlockSpec((1, ROW_TILE, D_MODEL), lambda b, i: (b, i, 0)),
        out_shape=jax.ShapeDtypeStruct((batch, SEQ, D_MODEL), F32),
        compiler_params=_params("parallel", "parallel"),
        name="final_norm",
    )(h, g)


def _rot_half_cols(w):
    return jnp.roll(w, -QK_ROPE // 2, axis=-1)


def _mla_weights(w_in, w_uq, w_ukv):
    w_q = w_in[:, :Q_LORA]
    w_kv = w_in[:, Q_LORA:Q_LORA + KV_LORA]
    w_kr = w_in[:, Q_LORA + KV_LORA:]
    z = jnp.zeros_like(w_kr)
    pair = lambda w: [w, z, z, w]
    win = jnp.concatenate([w_q, w_kv] + pair(w_kr) + pair(_rot_half_cols(w_kr)), axis=1)
    uq = w_uq.reshape(Q_LORA, MLA_HEADS, QK_NOPE + QK_ROPE)
    uq_n = uq[:, :, :QK_NOPE].reshape(Q_LORA, -1)
    uq_r = uq[:, :, QK_NOPE:]
    wuq = jnp.concatenate([uq_n, uq_r.reshape(Q_LORA, -1),
                           _rot_half_cols(uq_r).reshape(Q_LORA, -1)], axis=1)
    ukv = w_ukv.reshape(KV_LORA, MLA_HEADS, QK_NOPE + V_HEAD)
    wuk = ukv[:, :, :QK_NOPE].reshape(KV_LORA, -1)
    wvt = ukv[:, :, QK_NOPE:].reshape(KV_LORA, -1).T
    return win.astype(BF16), wuq.astype(BF16), wuk.astype(BF16), wvt.astype(BF16)


def _gate_block_diag(w):
    w = w.reshape(RNN_BLOCKS // 2, 2, RNN_BW, RNN_BW)
    z = jnp.zeros_like(w[:, 0])
    top = jnp.concatenate([w[:, 0], z], axis=2)
    bot = jnp.concatenate([z, w[:, 1]], axis=2)
    return jnp.concatenate([top, bot], axis=1).astype(BF16)


def _rope_tables():
    pos = jnp.maximum(jnp.arange(TP) - PAD, 0).astype(F32)
    inv_freq = ROPE_THETA ** (-jnp.arange(0, QK_ROPE, 2, dtype=F32) / QK_ROPE)
    ang = pos[:, None] * inv_freq[None, :]
    cos, sin = jnp.cos(ang), jnp.sin(ang)
    cos = jnp.concatenate([cos, cos, cos, cos], axis=1)
    sin = jnp.concatenate([-sin, sin, -sin, sin], axis=1)
    return cos, sin


def _mla_layer(h, g_mix, w_in, qg, kvg, w_uq, w_ukv, w_o_all, layer, cos, sin, meta=None):
    win, wuq, wuk, wvt = _mla_weights(w_in, w_uq, w_ukv)
    outs = _mla_proj(h, g_mix, win, qg, kvg, wuq, wuk, wvt, cos, sin, meta)
    if meta is not None:
        h = outs[5]
    return _proj_residual(h, _attention(*outs[:5]), w_o_all, layer)


def _lru_layer(h, g_mix, w_in_all, cw, cb, w_ga, b_ga, w_gx, b_gx, lam, w_o_all, layer):
    batch = h.shape[0]
    hf = h.reshape(batch * TP, D_MODEL)
    xb, yg = (a.reshape(batch, TP, D_RNN) for a in _lru_in(hf, g_mix, w_in_all, layer))
    mixed = _lru(xb, yg, cw, cb, _gate_block_diag(w_ga), b_ga, _gate_block_diag(w_gx), b_gx, lam)
    return _proj_residual(h, mixed, w_o_all, layer)


def kernel(x, meta_tokens, norm_mix, norm_ffn, norm_final, mla_w_in, mla_q_norm, mla_kv_norm, mla_w_uq, mla_w_ukv, mla_w_o, lru_w_in, lru_conv_w, lru_conv_b, lru_w_gate_a, lru_b_gate_a, lru_w_gate_x, lru_b_gate_x, lru_lambda, lru_w_o, ffn_w_gu, ffn_w_down):
    batch = x.shape[0]
    rows = batch * TP
    h = x
    cos, sin = _rope_tables()
    row_vec = lambda v: v.reshape(1, -1)
    flat = lambda a: a.reshape(rows, a.shape[-1])
    seq = lambda a: a.reshape(batch, TP, a.shape[-1])
    mla_w_o, lru_w_in, lru_w_o, ffn_w_gu, ffn_w_down = (
        w.astype(BF16) for w in (mla_w_o, lru_w_in, lru_w_o, ffn_w_gu, ffn_w_down))

    for layer in range(DEPTH):
        j = layer // N_MIXERS
        g_mix = row_vec(norm_mix[layer])
        if layer % N_MIXERS == 0:
            h = _mla_layer(h, g_mix, mla_w_in[j], row_vec(mla_q_norm[j]), row_vec(mla_kv_norm[j]),
                           mla_w_uq[j], mla_w_ukv[j], mla_w_o, j, cos, sin,
                           meta_tokens.astype(x.dtype) if layer == 0 else None)
        else:
            h = _lru_layer(h, g_mix, lru_w_in, lru_conv_w[j], row_vec(lru_conv_b[j]),
                           lru_w_gate_a[j], row_vec(lru_b_gate_a[j]), lru_w_gate_x[j],
                           row_vec(lru_b_gate_x[j]), row_vec(lru_lambda[j]), lru_w_o, j)
        h = seq(_ffn(flat(h), row_vec(norm_ffn[layer]), ffn_w_gu, ffn_w_down, layer))
    return _final_norm(h, row_vec(norm_final))
```

```python
import functools
import math

import jax
import jax.numpy as jnp
from jax import lax
from jax.experimental import pallas as pl
from jax.experimental.pallas import tpu as pltpu

D_MODEL = 2048
SEQ = 4096
CHUNK = 64
N_META = 16
DEPTH = 4
N_MIXERS = 2

MLA_HEADS = 16
Q_LORA = 512
KV_LORA = 512
QK_NOPE = 128
QK_ROPE = 64
V_HEAD = 128
ROPE_THETA = 10000.0

D_RNN = D_MODEL
RNN_BLOCKS = 16
RNN_BW = D_RNN // RNN_BLOCKS
CONV_W = 4
LRU_C = 8.0

D_FF = -(-8 * D_MODEL // (3 * 256)) * 256

RMS_EPS = 1e-6
NEG_BIG = -1e30

LANE = 128
MXU_DIM = 256
PAD = LANE - N_META
TP = PAD + N_META + SEQ
VMEM_LIMIT = 56 * 1024 * 1024

ROW_TILE = 512
SEQ_TILE = 384
LRU_IN_TILE = 512
PROJ_ROW_TILE = 768
FFN_ROW_TILE = 1056
FF_TILE = 512
ATT_TILE = 512
HEAD_GROUP = 4
L_ROWS = 16
GATE_TILE = 2 * RNN_BW
SCAN_BLOCK = 8

BF16 = jnp.bfloat16
F32 = jnp.float32


def _params(*sem):
    return pltpu.CompilerParams(dimension_semantics=sem, vmem_limit_bytes=VMEM_LIMIT)


def _rms(x, g):
    ms = jnp.mean(x * x, axis=-1, keepdims=True)
    return x * lax.rsqrt(ms + RMS_EPS) * g


def _dot(a, b):
    return jnp.dot(a, b, preferred_element_type=F32)


def _dot_nt(a, b):
    return lax.dot_general(a, b, (((1,), (1,)), ((), ())), preferred_element_type=F32)


def _const_spec(shape):
    return pl.BlockSpec(shape, lambda *_: (0,) * len(shape))


def _layer_spec(shape, layer):
    return pl.BlockSpec((None,) + tuple(shape), lambda *_: (layer,) + (0,) * len(shape))


def _mla_proj_kernel(*refs, from_frames):
    if from_frames:
        (x_ref, meta_ref, g_ref, win_ref, qg_ref, kvg_ref, wuq_ref, wuk_ref, wvt_ref,
         cos_ref, sin_ref, qn_ref, qr_ref, kn_ref, kr_ref, vt_ref, h0_ref) = refs
        frames = x_ref[0]
        first = jnp.concatenate([jnp.zeros((PAD, D_MODEL), F32), meta_ref[...],
                                 frames[:SEQ_TILE - PAD - N_META]], axis=0)
        h = jnp.where(pl.program_id(1) == 0, first, frames)
        h0_ref[0] = h
    else:
        (h_ref, g_ref, win_ref, qg_ref, kvg_ref, wuq_ref, wuk_ref, wvt_ref,
         cos_ref, sin_ref, qn_ref, qr_ref, kn_ref, kr_ref, vt_ref) = refs
        h = h_ref[0]
    scale = (QK_NOPE + QK_ROPE) ** -0.5 * math.log2(math.e)
    hn = _rms(h, g_ref[...]).astype(BF16)
    proj = _dot(hn, win_ref[...])
    cq = _rms(proj[:, :Q_LORA], qg_ref[...]).astype(BF16)
    ckv = _rms(proj[:, Q_LORA:Q_LORA + KV_LORA], kvg_ref[...]).astype(BF16)
    cos = cos_ref[...]
    sin = sin_ref[...]
    o = Q_LORA + KV_LORA
    for p in range(2):
        raw = proj[:, o + p * LANE:o + (p + 1) * LANE]
        rot = proj[:, o + (2 + p) * LANE:o + (3 + p) * LANE]
        kr_ref[0, :, p * LANE:(p + 1) * LANE] = (raw * cos + rot * sin).astype(BF16)
    q = _dot(cq, wuq_ref[...])
    hn_w = MLA_HEADS * QK_NOPE
    hr_w = MLA_HEADS * QK_ROPE
    qn_ref[0] = (q[:, :hn_w] * scale).astype(BF16)
    for p in range(hr_w // LANE):
        raw = q[:, hn_w + p * LANE:hn_w + (p + 1) * LANE]
        rot = q[:, hn_w + hr_w + p * LANE:hn_w + hr_w + (p + 1) * LANE]
        qr_ref[0, :, p * LANE:(p + 1) * LANE] = ((raw * cos + rot * sin) * scale).astype(BF16)
    kn_ref[0] = _dot(ckv, wuk_ref[...]).astype(BF16)
    vt_ref[...] = _dot_nt(wvt_ref[...], ckv).astype(BF16)


def _mla_proj(h, g, win, qg, kvg, wuq, wuk, wvt, cos, sin, meta=None):
    batch = h.shape[0]
    tm = SEQ_TILE
    n_t = TP // tm
    hn_w = MLA_HEADS * QK_NOPE
    hr_w = MLA_HEADS * QK_ROPE
    seq_blk = lambda w: pl.BlockSpec((1, tm, w), lambda b, t: (b, t, 0))
    consts = (g, win, qg, kvg, wuq, wuk, wvt)
    out_w = (hn_w, hr_w, hn_w, 2 * LANE)
    out_specs = ([seq_blk(w) for w in out_w]
                 + [pl.BlockSpec((hn_w, tm), lambda b, t: (0, b * n_t + t))])
    out_shape = ([jax.ShapeDtypeStruct((batch, TP, w), BF16) for w in out_w]
                 + [jax.ShapeDtypeStruct((hn_w, batch * TP), BF16)])
    if meta is None:
        tokens, token_specs = (h,), [seq_blk(D_MODEL)]
    else:
        first_frame = lambda t: pl.multiple_of(jnp.maximum(t * tm - (PAD + N_META), 0), LANE)
        frame_spec = pl.BlockSpec((pl.Element(1), pl.Element(tm), pl.Element(D_MODEL)),
                                  lambda b, t: (b, first_frame(t), 0))
        tokens, token_specs = (h, meta), [frame_spec, _const_spec(meta.shape)]
        out_specs.append(seq_blk(D_MODEL))
        out_shape.append(jax.ShapeDtypeStruct((batch, TP, D_MODEL), F32))
    return pl.pallas_call(
        functools.partial(_mla_proj_kernel, from_frames=meta is not None),
        grid=(batch, n_t),
        in_specs=token_specs + [_const_spec(c.shape) for c in consts]
        + [pl.BlockSpec((tm, LANE), lambda b, t: (t, 0))] * 2,
        out_specs=out_specs,
        out_shape=out_shape,
        compiler_params=_params("parallel", "parallel"),
        name="mla_proj",
    )(*tokens, *consts, cos, sin)


def _attn_kernel(qn_ref, qr_ref, qnm_ref, qrm_ref, kn_ref, kr_ref, vt_ref, o_ref,
                 s_sc, smax_sc, m_sc, acc_sc):
    i = pl.program_id(2)
    t = ATT_TILE
    base = PAD + N_META
    heads = range(HEAD_GROUP)
    h_cols = lambda hh: slice(hh * LANE, (hh + 1) * LANE)
    r_cols = lambda hh: slice((hh // 2) * LANE, (hh // 2 + 1) * LANE)
    chains = []
    for hh in heads:
        q = jnp.concatenate([qn_ref[0, :, h_cols(hh)], qr_ref[0, :, r_cols(hh)]], axis=1)
        for c in range(t // MXU_DIM):
            chains.append((hh, c, q[c * MXU_DIM:(c + 1) * MXU_DIM, :]))

    def keys(hh, start, size):
        half = slice((hh % 2) * LANE, (hh % 2 + 1) * LANE)
        return jnp.concatenate([kn_ref[0, pl.ds(start, size), h_cols(hh)],
                                kr_ref[0, pl.ds(start, size), half]], axis=1)

    def values_t(hh, start, size):
        return jnp.concatenate([vt_ref[hh * V_HEAD:(hh + 1) * V_HEAD, pl.ds(start, size)],
                                jnp.ones((L_ROWS, size), BF16)], axis=0)

    def frame_start(j):
        return pl.multiple_of(base + j * t, LANE)

    def issue_scores(j, buf):
        start = frame_start(j)
        for n, (hh, c, q) in enumerate(chains):
            s = _dot_nt(keys(hh, start, t), q)
            s_sc[buf, n] = s
            smax_sc[buf, n] = jnp.max(s, axis=0, keepdims=True)

    def softmax_pv(n, s, v_t, first=False, s_max=None):
        m_new = jnp.max(s, axis=0, keepdims=True) if s_max is None else s_max
        if first:
            p = jnp.exp2(s - m_new)
            acc_sc[n] = _dot(v_t, p.astype(BF16))
        else:
            m_prev = m_sc[n]
            m_new = jnp.maximum(m_prev, m_new)
            alpha = jnp.exp2(m_prev - m_new)
            p = jnp.exp2(s - m_new)
            acc_sc[n] = alpha * acc_sc[n] + _dot(v_t, p.astype(BF16))
        m_sc[n] = m_new

    def consume(j, buf, mask=None):
        start = frame_start(j)
        for n, (hh, c, _) in enumerate(chains):
            s = s_sc[buf, n]
            s_max = None
            if mask is not None:
                s = jnp.where(mask(c), s, NEG_BIG)
            else:
                s_max = smax_sc[buf, n]
            softmax_pv(n, s, values_t(hh, start, t), s_max=s_max)

    meta_valid = lax.broadcasted_iota(jnp.int32, (base, 1), 0) >= PAD
    meta_scores = [_dot_nt(keys(hh, 0, base), q) for hh, c, q in chains]
    issue_scores(0, 0)
    for n, (hh, c, _) in enumerate(chains):
        softmax_pv(n, jnp.where(meta_valid, meta_scores[n], NEG_BIG), values_t(hh, 0, base), first=True)

    def pair(jj, carry):
        j = 2 * jj
        issue_scores(j + 1, 1)
        consume(j, 0)
        issue_scores(j + 2, 0)
        consume(j + 1, 1)
        return carry

    lax.fori_loop(0, i // 2, pair, 0)

    def diag_mask(c):
        kk = lax.broadcasted_iota(jnp.int32, (t, 1), 0)
        qq = c * MXU_DIM + lax.broadcasted_iota(jnp.int32, (1, MXU_DIM), 1)
        return kk < (qq // CHUNK + 1) * CHUNK

    @pl.when(i % 2 == 1)
    def _():
        issue_scores(i, 1)
        consume(i - 1, 0)
        consume(i, 1, diag_mask)

    @pl.when(i % 2 == 0)
    def _():
        consume(i, 0, diag_mask)

    row0 = frame_start(i)
    for n, (hh, c, _) in enumerate(chains):
        o_t = acc_sc[n, 0:V_HEAD] / acc_sc[n, V_HEAD:V_HEAD + 1]
        rows = pl.ds(pl.multiple_of(row0 + c * MXU_DIM, LANE), MXU_DIM)
        o_ref[0, rows, h_cols(hh)] = o_t.T.astype(BF16)

    @pl.when(i == 0)
    def _():
        for hh in heads:
            qm = jnp.concatenate([qnm_ref[0, :, h_cols(hh)], qrm_ref[0, :, r_cols(hh)]], axis=1)
            s = jnp.where(meta_valid, _dot_nt(keys(hh, 0, base), qm), NEG_BIG)
            p = jnp.exp2(s - jnp.max(s, axis=0, keepdims=True))
            pv = _dot(values_t(hh, 0, base), p.astype(BF16))
            o_t = pv[0:V_HEAD] / pv[V_HEAD:V_HEAD + 1]
            o_ref[0, 0:base, h_cols(hh)] = o_t.T.astype(BF16)


def _attention(qn, qr, kn, kr, vt):
    batch = qn.shape[0]
    t = ATT_TILE
    base = PAD + N_META
    gw = HEAD_GROUP * LANE
    n_chain = HEAD_GROUP * (t // MXU_DIM)

    def frames(width):
        shape = (pl.Element(1), pl.Element(t), pl.Element(width))
        return pl.BlockSpec(shape, lambda b, g, i: (b, pl.multiple_of(base + i * t, LANE),
                                                    pl.multiple_of(g * width, LANE)))

    return pl.pallas_call(
        _attn_kernel,
        grid=(batch, MLA_HEADS // HEAD_GROUP, SEQ // t),
        in_specs=[frames(gw), frames(gw // 2),
                  pl.BlockSpec((1, base, gw), lambda b, g, i: (b, 0, g)),
                  pl.BlockSpec((1, base, gw // 2), lambda b, g, i: (b, 0, g)),
                  pl.BlockSpec((1, TP, gw), lambda b, g, i: (b, 0, g)),
                  pl.BlockSpec((1, TP, 2 * LANE), lambda b, g, i: (b, 0, 0)),
                  pl.BlockSpec((gw, TP), lambda b, g, i: (g, b))],
        out_specs=pl.BlockSpec((1, TP, gw), lambda b, g, i: (b, 0, g)),
        out_shape=jax.ShapeDtypeStruct((batch, TP, MLA_HEADS * V_HEAD), BF16),
        scratch_shapes=[pltpu.VMEM((2, n_chain, t, MXU_DIM), F32),
                        pltpu.VMEM((2, n_chain, 1, MXU_DIM), F32),
                        pltpu.VMEM((n_chain, 1, MXU_DIM), F32),
                        pltpu.VMEM((n_chain, V_HEAD + L_ROWS, MXU_DIM), F32)],
        compiler_params=_params("parallel", "parallel", "arbitrary"),
        name="mla_attention",
    )(qn, qr, qn, qr, kn, kr, vt)


def _proj_residual_kernel(h_ref, a_ref, w_ref, o_ref):
    o_ref[...] = h_ref[...] + _dot(a_ref[...], w_ref[...])


def _proj_residual(h, a, w, layer):
    batch = h.shape[0]
    rows = batch * TP
    tm = PROJ_ROW_TILE
    blk = lambda width: pl.BlockSpec((tm, width), lambda i: (i, 0))
    return pl.pallas_call(
        _proj_residual_kernel,
        grid=(rows // tm,),
        in_specs=[blk(D_MODEL), blk(a.shape[2]), _layer_spec(w.shape[1:], layer)],
        out_specs=blk(D_MODEL),
        out_shape=jax.ShapeDtypeStruct((rows, D_MODEL), F32),
        compiler_params=_params("parallel"),
        name="proj_residual",
    )(h.reshape(rows, D_MODEL), a.reshape(rows, a.shape[2]), w).reshape(h.shape)


def _ffn_kernel(h_ref, g_ref, wg_ref, wu_ref, wd_ref, o_ref, hn_sc):
    def swiglu_tile(hn):
        gate = _dot(hn, wg_ref[...])
        up = _dot(hn, wu_ref[...])
        act = (gate * jax.nn.sigmoid(gate) * up).astype(BF16)
        return _dot(act, wd_ref[...])

    @pl.when(pl.program_id(1) == 0)
    def _():
        half = FFN_ROW_TILE // 2
        for r in range(2):
            rows = slice(r * half, (r + 1) * half)
            x = h_ref[rows, :]
            hn = _rms(x, g_ref[...]).astype(BF16)
            hn_sc[rows, :] = hn
            o_ref[rows, :] = x + swiglu_tile(hn)

    @pl.when(pl.program_id(1) > 0)
    def _():
        o_ref[...] += swiglu_tile(hn_sc[...])


def _ffn(h, g, w_gu, w_down, layer):
    rows = h.shape[0]
    tm = FFN_ROW_TILE
    n_f = D_FF // FF_TILE
    return pl.pallas_call(
        _ffn_kernel,
        grid=(rows // tm, n_f),
        in_specs=[pl.BlockSpec((tm, D_MODEL), lambda i, j: (i, 0)),
                  pl.BlockSpec((1, D_MODEL), lambda i, j: (0, 0)),
                  pl.BlockSpec((None, D_MODEL, FF_TILE), lambda i, j: (layer, 0, j)),
                  pl.BlockSpec((None, D_MODEL, FF_TILE), lambda i, j: (layer, 0, j + n_f)),
                  pl.BlockSpec((None, FF_TILE, D_MODEL), lambda i, j: (layer, j, 0))],
        out_specs=pl.BlockSpec((tm, D_MODEL), lambda i, j: (i, 0)),
        out_shape=jax.ShapeDtypeStruct(h.shape, F32),
        scratch_shapes=[pltpu.VMEM((tm, D_MODEL), BF16)],
        compiler_params=_params("parallel", "arbitrary"),
        name="ffn",
    )(h, g, w_gu, w_gu, w_down)


def _lru_in_kernel(h_ref, g_ref, w_ref, xb_ref, yg_ref):
    half = LRU_IN_TILE // 2
    for r in range(2):
        rows = slice(r * half, (r + 1) * half)
        hn = _rms(h_ref[rows, :], g_ref[...]).astype(BF16)
        xb_ref[rows, :] = _dot(hn, w_ref[:, :D_RNN])
        y = _dot(hn, w_ref[:, D_RNN:])
        c = (2.0 / jnp.pi) ** 0.5
        yg_ref[rows, :] = 0.5 * y * (1.0 + jnp.tanh(c * (y + 0.044715 * (y * y * y))))


def _lru_in(h, g, w, layer):
    rows = h.shape[0]
    tm = LRU_IN_TILE
    row = lambda width: pl.BlockSpec((tm, width), lambda i: (i, 0))
    return pl.pallas_call(
        _lru_in_kernel,
        grid=(rows // tm,),
        in_specs=[row(D_MODEL), _const_spec(g.shape), _layer_spec(w.shape[1:], layer)],
        out_specs=[row(D_RNN), row(D_RNN)],
        out_shape=[jax.ShapeDtypeStruct((rows, D_RNN), F32)] * 2,
        compiler_params=_params("parallel"),
        name="lru_in",
    )(h, g, w)


def _lru_kernel(xb_ref, yg_ref, cw_ref, cb_ref, wga_ref, bga_ref, wgx_ref, bgx_ref, lam_ref,
                o_ref, x_sc, tail_sc, a_sc, b_sc, hs_sc, h_sc):
    t = pl.program_id(1)
    tt = SEQ_TILE
    halo = 8

    x_sc[...] = xb_ref[0]

    @pl.when(t == 0)
    def _():
        x_sc[0:PAD, :] = jnp.zeros((PAD, D_RNN), F32)
        tail_sc[...] = jnp.zeros(tail_sc.shape, F32)
        h_sc[...] = jnp.zeros(h_sc.shape, F32)

    neg_lam = -lam_ref[...]
    softplus = jnp.maximum(neg_lam, 0.0) + jnp.log1p(jnp.exp(-jnp.abs(neg_lam)))
    neg_log_a_rate = LRU_C * softplus
    a_exp2_rate = -math.log2(math.e) * neg_log_a_rate
    for g in range(D_RNN // GATE_TILE):
        cols = slice(g * GATE_TILE, (g + 1) * GATE_TILE)
        x = x_sc[:, cols]
        taps = [cw_ref[k:k + 1, cols] for k in range(CONV_W)]
        xc = cb_ref[:, cols] + taps[CONV_W - 1] * x
        for d in range(1, CONV_W):
            xc = xc + taps[CONV_W - 1 - d] * pltpu.roll(x, d, axis=0)
        head = jnp.concatenate([tail_sc[:, cols], x[0:halo]], axis=0)
        xc_head = cb_ref[:, cols]
        for k in range(CONV_W):
            o = halo - (CONV_W - 1) + k
            xc_head = xc_head + taps[k] * head[o:o + halo]
        xc = jnp.concatenate([xc_head, xc[halo:]], axis=0)
        xc16 = xc.astype(BF16)
        r = jax.nn.sigmoid(_dot(xc16, wga_ref[g]) + bga_ref[:, cols])
        ig = jax.nn.sigmoid(_dot(xc16, wgx_ref[g]) + bgx_ref[:, cols])
        a = jnp.exp2(r * a_exp2_rate[:, cols])
        a_sc[:, :, cols] = a.reshape(tt // SCAN_BLOCK, SCAN_BLOCK, GATE_TILE)
        gain = jnp.sqrt(jnp.tanh(r * neg_log_a_rate[:, cols]) * (1.0 + a * a))
        b_sc[:, :, cols] = (gain * (ig * xc)).reshape(tt // SCAN_BLOCK, SCAN_BLOCK, GATE_TILE)

    tail_sc[...] = x_sc[tt - halo:tt, :]

    @pl.when(t == 0)
    def _():
        b_sc[0:PAD // SCAN_BLOCK] = jnp.zeros((PAD // SCAN_BLOCK, SCAN_BLOCK, D_RNN), F32)

    def body(blk, h):
        a_cum = b_cum = h_out = None
        for k in range(SCAN_BLOCK):
            a_k = a_sc[blk, k:k + 1, :]
            b_k = b_sc[blk, k:k + 1, :]
            if k == 0:
                a_cum, b_cum = a_k, b_k
            else:
                a_cum, b_cum = a_k * a_cum, a_k * b_cum + b_k
            h_out = a_cum * h + b_cum
            hs_sc[blk, k:k + 1, :] = h_out
        return h_out

    h_sc[...] = lax.fori_loop(0, tt // SCAN_BLOCK, body, h_sc[...], unroll=4)
    o_ref[0] = (hs_sc[...].reshape(tt, D_RNN) * yg_ref[0]).astype(BF16)


def _lru(xb, yg, cw, cb, wga, bga, wgx, bgx, lam):
    batch = xb.shape[0]
    tt = SEQ_TILE
    blk = pl.BlockSpec((1, tt, D_RNN), lambda b, t: (b, t, 0))
    consts = (cw, cb, wga, bga, wgx, bgx, lam)
    return pl.pallas_call(
        _lru_kernel,
        grid=(batch, TP // tt),
        in_specs=[blk, blk] + [_const_spec(c.shape) for c in consts],
        out_specs=blk,
        out_shape=jax.ShapeDtypeStruct(xb.shape, BF16),
        scratch_shapes=[pltpu.VMEM((tt, D_RNN), F32), pltpu.VMEM((8, D_RNN), F32)]
        + [pltpu.VMEM((tt // SCAN_BLOCK, SCAN_BLOCK, D_RNN), F32)] * 3
        + [pltpu.VMEM((1, D_RNN), F32)],
        compiler_params=_params("parallel", "arbitrary"),
        name="rglru",
    )(xb, yg, *consts)


def _final_norm_kernel(h_ref, g_ref, o_ref):
    o_ref[0] = _rms(h_ref[0], g_ref[...])


def _final_norm(h, g):
    batch = h.shape[0]
    return pl.pallas_call(
        _final_norm_kernel,
        grid=(batch, SEQ // ROW_TILE),
        in_specs=[pl.BlockSpec((pl.Element(1), pl.Element(ROW_TILE), pl.Element(D_MODEL)),
                               lambda b, i: (b, pl.multiple_of(PAD + N_META + i * ROW_TILE, LANE), 0)),
                  _const_spec(g.shape)],
        out_specs=pl.BlockSpec((1, ROW_TILE, D_MODEL), lambda b, i: (b, i, 0)),
        out_shape=jax.ShapeDtypeStruct((batch, SEQ, D_MODEL), F32),
        compiler_params=_params("parallel", "parallel"),
        name="final_norm",
    )(h, g)


def _rot_half_cols(w):
    return jnp.roll(w, -QK_ROPE // 2, axis=-1)


def _mla_weights(w_in, w_uq, w_ukv):
    w_q = w_in[:, :Q_LORA]
    w_kv = w_in[:, Q_LORA:Q_LORA + KV_LORA]
    w_kr = w_in[:, Q_LORA + KV_LORA:]
    z = jnp.zeros_like(w_kr)
    pair = lambda w: [w, z, z, w]
    win = jnp.concatenate([w_q, w_kv] + pair(w_kr) + pair(_rot_half_cols(w_kr)), axis=1)
    uq = w_uq.reshape(Q_LORA, MLA_HEADS, QK_NOPE + QK_ROPE)
    uq_n = uq[:, :, :QK_NOPE].reshape(Q_LORA, -1)
    uq_r = uq[:, :, QK_NOPE:]
    wuq = jnp.concatenate([uq_n, uq_r.reshape(Q_LORA, -1),
                           _rot_half_cols(uq_r).reshape(Q_LORA, -1)], axis=1)
    ukv = w_ukv.reshape(KV_LORA, MLA_HEADS, QK_NOPE + V_HEAD)
    wuk = ukv[:, :, :QK_NOPE].reshape(KV_LORA, -1)
    wvt = ukv[:, :, QK_NOPE:].reshape(KV_LORA, -1).T
    return win.astype(BF16), wuq.astype(BF16), wuk.astype(BF16), wvt.astype(BF16)


def _gate_block_diag(w):
    w = w.reshape(RNN_BLOCKS // 2, 2, RNN_BW, RNN_BW)
    z = jnp.zeros_like(w[:, 0])
    top = jnp.concatenate([w[:, 0], z], axis=2)
    bot = jnp.concatenate([z, w[:, 1]], axis=2)
    return jnp.concatenate([top, bot], axis=1).astype(BF16)


def _rope_tables():
    pos = jnp.maximum(jnp.arange(TP) - PAD, 0).astype(F32)
    inv_freq = ROPE_THETA ** (-jnp.arange(0, QK_ROPE, 2, dtype=F32) / QK_ROPE)
    ang = pos[:, None] * inv_freq[None, :]
    cos, sin = jnp.cos(ang), jnp.sin(ang)
    cos = jnp.concatenate([cos, cos, cos, cos], axis=1)
    sin = jnp.concatenate([-sin, sin, -sin, sin], axis=1)
    return cos, sin


def _mla_layer(h, g_mix, w_in, qg, kvg, w_uq, w_ukv, w_o_all, layer, cos, sin, meta=None):
    win, wuq, wuk, wvt = _mla_weights(w_in, w_uq, w_ukv)
    outs = _mla_proj(h, g_mix, win, qg, kvg, wuq, wuk, wvt, cos, sin, meta)
    if meta is not None:
        h = outs[5]
    return _proj_residual(h, _attention(*outs[:5]), w_o_all, layer)


def _lru_layer(h, g_mix, w_in_all, cw, cb, w_ga, b_ga, w_gx, b_gx, lam, w_o_all, layer):
    batch = h.shape[0]
    hf = h.reshape(batch * TP, D_MODEL)
    xb, yg = (a.reshape(batch, TP, D_RNN) for a in _lru_in(hf, g_mix, w_in_all, layer))
    mixed = _lru(xb, yg, cw, cb, _gate_block_diag(w_ga), b_ga, _gate_block_diag(w_gx), b_gx, lam)
    return _proj_residual(h, mixed, w_o_all, layer)


def kernel(x, meta_tokens, norm_mix, norm_ffn, norm_final, mla_w_in, mla_q_norm, mla_kv_norm, mla_w_uq, mla_w_ukv, mla_w_o, lru_w_in, lru_conv_w, lru_conv_b, lru_w_gate_a, lru_b_gate_a, lru_w_gate_x, lru_b_gate_x, lru_lambda, lru_w_o, ffn_w_gu, ffn_w_down):
    batch = x.shape[0]
    rows = batch * TP
    h = x
    cos, sin = _rope_tables()
    row_vec = lambda v: v.reshape(1, -1)
    flat = lambda a: a.reshape(rows, a.shape[-1])
    seq = lambda a: a.reshape(batch, TP, a.shape[-1])
    mla_w_o, lru_w_in, lru_w_o, ffn_w_gu, ffn_w_down = (
        w.astype(BF16) for w in (mla_w_o, lru_w_in, lru_w_o, ffn_w_gu, ffn_w_down))

    for layer in range(DEPTH):
        j = layer // N_MIXERS
        g_mix = row_vec(norm_mix[layer])
        if layer % N_MIXERS == 0:
            h = _mla_layer(h, g_mix, mla_w_in[j], row_vec(mla_q_norm[j]), row_vec(mla_kv_norm[j]),
                           mla_w_uq[j], mla_w_ukv[j], mla_w_o, j, cos, sin,
                           meta_tokens.astype(x.dtype) if layer == 0 else None)
        else:
            h = _lru_layer(h, g_mix, lru_w_in, lru_conv_w[j], row_vec(lru_conv_b[j]),
                           lru_w_gate_a[j], row_vec(lru_b_gate_a[j]), lru_w_gate_x[j],
                           row_vec(lru_b_gate_x[j]), row_vec(lru_lambda[j]), lru_w_o, j)
        h = seq(_ffn(flat(h), row_vec(norm_ffn[layer]), ffn_w_gu, ffn_w_down, layer))
    return _final_norm(h, row_vec(norm_final))
```

```python
import functools
import math

import jax
import jax.numpy as jnp
from jax import lax
from jax.experimental import pallas as pl
from jax.experimental.pallas import tpu as pltpu

D_MODEL = 2048
SEQ = 4096
CHUNK = 64
N_META = 16
DEPTH = 4
N_MIXERS = 2

MLA_HEADS = 16
Q_LORA = 512
KV_LORA = 512
QK_NOPE = 128
QK_ROPE = 64
V_HEAD = 128
ROPE_THETA = 10000.0

D_RNN = D_MODEL
RNN_BLOCKS = 16
RNN_BW = D_RNN // RNN_BLOCKS
CONV_W = 4
LRU_C = 8.0

D_FF = -(-8 * D_MODEL // (3 * 256)) * 256

RMS_EPS = 1e-6
NEG_BIG = -1e30

LANE = 128
MXU_DIM = 256
PAD = LANE - N_META
TP = PAD + N_META + SEQ
VMEM_LIMIT = 56 * 1024 * 1024

ROW_TILE = 512
SEQ_TILE = 384
LRU_IN_TILE = 512
PROJ_ROW_TILE = 768
FFN_ROW_TILE = 1056
FF_TILE = 512
ATT_TILE = 512
HEAD_GROUP = 4
L_ROWS = 16
GATE_TILE = 2 * RNN_BW
SCAN_BLOCK = 8

BF16 = jnp.bfloat16
F32 = jnp.float32


def _params(*sem):
    return pltpu.CompilerParams(dimension_semantics=sem, vmem_limit_bytes=VMEM_LIMIT)


def _rms(x, g):
    ms = jnp.mean(x * x, axis=-1, keepdims=True)
    return x * lax.rsqrt(ms + RMS_EPS) * g


def _dot(a, b):
    return jnp.dot(a, b, preferred_element_type=F32)


def _dot_nt(a, b):
    return lax.dot_general(a, b, (((1,), (1,)), ((), ())), preferred_element_type=F32)


def _const_spec(shape):
    return pl.BlockSpec(shape, lambda *_: (0,) * len(shape))


def _layer_spec(shape, layer):
    return pl.BlockSpec((None,) + tuple(shape), lambda *_: (layer,) + (0,) * len(shape))


def _mla_proj_kernel(*refs, from_frames):
    if from_frames:
        (x_ref, meta_ref, g_ref, win_ref, qg_ref, kvg_ref, wuq_ref, wuk_ref, wvt_ref,
         cos_ref, sin_ref, qn_ref, qr_ref, kn_ref, kr_ref, vt_ref, h0_ref) = refs
        frames = x_ref[0]
        first = jnp.concatenate([jnp.zeros((PAD, D_MODEL), F32), meta_ref[...],
                                 frames[:SEQ_TILE - PAD - N_META]], axis=0)
        h = jnp.where(pl.program_id(1) == 0, first, frames)
        h0_ref[0] = h
    else:
        (h_ref, g_ref, win_ref, qg_ref, kvg_ref, wuq_ref, wuk_ref, wvt_ref,
         cos_ref, sin_ref, qn_ref, qr_ref, kn_ref, kr_ref, vt_ref) = refs
        h = h_ref[0]
    scale = (QK_NOPE + QK_ROPE) ** -0.5 * math.log2(math.e)
    hn = _rms(h, g_ref[...]).astype(BF16)
    proj = _dot(hn, win_ref[...])
    cq = _rms(proj[:, :Q_LORA], qg_ref[...]).astype(BF16)
    ckv = _rms(proj[:, Q_LORA:Q_LORA + KV_LORA], kvg_ref[...]).astype(BF16)
    cos = cos_ref[...]
    sin = sin_ref[...]
    o = Q_LORA + KV_LORA
    for p in range(2):
        raw = proj[:, o + p * LANE:o + (p + 1) * LANE]
        rot = proj[:, o + (2 + p) * LANE:o + (3 + p) * LANE]
        kr_ref[0, :, p * LANE:(p + 1) * LANE] = (raw * cos + rot * sin).astype(BF16)
    q = _dot(cq, wuq_ref[...])
    hn_w = MLA_HEADS * QK_NOPE
    hr_w = MLA_HEADS * QK_ROPE
    qn_ref[0] = (q[:, :hn_w] * scale).astype(BF16)
    for p in range(hr_w // LANE):
        raw = q[:, hn_w + p * LANE:hn_w + (p + 1) * LANE]
        rot = q[:, hn_w + hr_w + p * LANE:hn_w + hr_w + (p + 1) * LANE]
        qr_ref[0, :, p * LANE:(p + 1) * LANE] = ((raw * cos + rot * sin) * scale).astype(BF16)
    kn_ref[0] = _dot(ckv, wuk_ref[...]).astype(BF16)
    vt_ref[...] = _dot_nt(wvt_ref[...], ckv).astype(BF16)


def _mla_proj(h, g, win, qg, kvg, wuq, wuk, wvt, cos, sin, meta=None):
    batch = h.shape[0]
    tm = SEQ_TILE
    n_t = TP // tm
    hn_w = MLA_HEADS * QK_NOPE
    hr_w = MLA_HEADS * QK_ROPE
    seq_blk = lambda w: pl.BlockSpec((1, tm, w), lambda b, t: (b, t, 0))
    consts = (g, win, qg, kvg, wuq, wuk, wvt)
    out_w = (hn_w, hr_w, hn_w, 2 * LANE)
    out_specs = ([seq_blk(w) for w in out_w]
                 + [pl.BlockSpec((hn_w, tm), lambda b, t: (0, b * n_t + t))])
    out_shape = ([jax.ShapeDtypeStruct((batch, TP, w), BF16) for w in out_w]
                 + [jax.ShapeDtypeStruct((hn_w, batch * TP), BF16)])
    if meta is None:
        tokens, token_specs = (h,), [seq_blk(D_MODEL)]
    else:
        first_frame = lambda t: pl.multiple_of(jnp.maximum(t * tm - (PAD + N_META), 0), LANE)
        frame_spec = pl.BlockSpec((pl.Element(1), pl.Element(tm), pl.Element(D_MODEL)),
                                  lambda b, t: (b, first_frame(t), 0))
        tokens, token_specs = (h, meta), [frame_spec, _const_spec(meta.shape)]
        out_specs.append(seq_blk(D_MODEL))
        out_shape.append(jax.ShapeDtypeStruct((batch, TP, D_MODEL), F32))
    return pl.pallas_call(
        functools.partial(_mla_proj_kernel, from_frames=meta is not None),
        grid=(batch, n_t),
        in_specs=token_specs + [_const_spec(c.shape) for c in consts]
        + [pl.BlockSpec((tm, LANE), lambda b, t: (t, 0))] * 2,
        out_specs=out_specs,
        out_shape=out_shape,
        compiler_params=_params("parallel", "parallel"),
        name="mla_proj",
    )(*tokens, *consts, cos, sin)


def _attn_kernel(qn_ref, qr_ref, qnm_ref, qrm_ref, kn_ref, kr_ref, vt_ref, o_ref,
                 s_sc, smax_sc, m_sc, acc_sc):
    i = pl.program_id(2)
    t = ATT_TILE
    base = PAD + N_META
    heads = range(HEAD_GROUP)
    h_cols = lambda hh: slice(hh * LANE, (hh + 1) * LANE)
    r_cols = lambda hh: slice((hh // 2) * LANE, (hh // 2 + 1) * LANE)
    chains = []
    for hh in heads:
        q = jnp.concatenate([qn_ref[0, :, h_cols(hh)], qr_ref[0, :, r_cols(hh)]], axis=1)
        for c in range(t // MXU_DIM):
            chains.append((hh, c, q[c * MXU_DIM:(c + 1) * MXU_DIM, :]))

    def keys(hh, start, size):
        half = slice((hh % 2) * LANE, (hh % 2 + 1) * LANE)
        return jnp.concatenate([kn_ref[0, pl.ds(start, size), h_cols(hh)],
                                kr_ref[0, pl.ds(start, size), half]], axis=1)

    def values_t(hh, start, size):
        return jnp.concatenate([vt_ref[hh * V_HEAD:(hh + 1) * V_HEAD, pl.ds(start, size)],
                                jnp.ones((L_ROWS, size), BF16)], axis=0)

    def frame_start(j):
        return pl.multiple_of(base + j * t, LANE)

    def issue_scores(j, buf):
        start = frame_start(j)
        for n, (hh, c, q) in enumerate(chains):
            s = _dot_nt(keys(hh, start, t), q)
            s_sc[buf, n] = s
            smax_sc[buf, n] = jnp.max(s, axis=0, keepdims=True)

    def softmax_pv(n, s, v_t, first=False, s_max=None):
        m_new = jnp.max(s, axis=0, keepdims=True) if s_max is None else s_max
        if first:
            p = jnp.exp2(s - m_new)
            acc_sc[n] = _dot(v_t, p.astype(BF16))
        else:
            m_prev = m_sc[n]
            m_new = jnp.maximum(m_prev, m_new)
            alpha = jnp.exp2(m_prev - m_new)
            p = jnp.exp2(s - m_new)
            acc_sc[n] = alpha * acc_sc[n] + _dot(v_t, p.astype(BF16))
        m_sc[n] = m_new

    def consume(j, buf, mask=None):
        start = frame_start(j)
        for n, (hh, c, _) in enumerate(chains):
            s = s_sc[buf, n]
            s_max = None
            if mask is not None:
                s = jnp.where(mask(c), s, NEG_BIG)
            else:
                s_max = smax_sc[buf, n]
            softmax_pv(n, s, values_t(hh, start, t), s_max=s_max)

    meta_valid = lax.broadcasted_iota(jnp.int32, (base, 1), 0) >= PAD
    meta_scores = [_dot_nt(keys(hh, 0, base), q) for hh, c, q in chains]
    issue_scores(0, 0)
    for n, (hh, c, _) in enumerate(chains):
        softmax_pv(n, jnp.where(meta_valid, meta_scores[n], NEG_BIG), values_t(hh, 0, base), first=True)

    def pair(jj, carry):
        j = 2 * jj
        issue_scores(j + 1, 1)
        consume(j, 0)
        issue_scores(j + 2, 0)
        consume(j + 1, 1)
        return carry

    lax.fori_loop(0, i // 2, pair, 0)

    def diag_mask(c):
        kk = lax.broadcasted_iota(jnp.int32, (t, 1), 0)
        qq = c * MXU_DIM + lax.broadcasted_iota(jnp.int32, (1, MXU_DIM), 1)
        return kk < (qq // CHUNK + 1) * CHUNK

    @pl.when(i % 2 == 1)
    def _():
        issue_scores(i, 1)
        consume(i - 1, 0)
        consume(i, 1, diag_mask)

    @pl.when(i % 2 == 0)
    def _():
        consume(i, 0, diag_mask)

    row0 = frame_start(i)
    for n, (hh, c, _) in enumerate(chains):
        o_t = acc_sc[n, 0:V_HEAD] / acc_sc[n, V_HEAD:V_HEAD + 1]
        rows = pl.ds(pl.multiple_of(row0 + c * MXU_DIM, LANE), MXU_DIM)
        o_ref[0, rows, h_cols(hh)] = o_t.T.astype(BF16)

    @pl.when(i == 0)
    def _():
        for hh in heads:
            qm = jnp.concatenate([qnm_ref[0, :, h_cols(hh)], qrm_ref[0, :, r_cols(hh)]], axis=1)
            s = jnp.where(meta_valid, _dot_nt(keys(hh, 0, base), qm), NEG_BIG)
            p = jnp.exp2(s - jnp.max(s, axis=0, keepdims=True))
            pv = _dot(values_t(hh, 0, base), p.astype(BF16))
            o_t = pv[0:V_HEAD] / pv[V_HEAD:V_HEAD + 1]
            o_ref[0, 0:base, h_cols(hh)] = o_t.T.astype(BF16)


def _attention(qn, qr, kn, kr, vt):
    batch = qn.shape[0]
    t = ATT_TILE
    base = PAD + N_META
    gw = HEAD_GROUP * LANE
    n_chain = HEAD_GROUP * (t // MXU_DIM)

    def frames(width):
        shape = (pl.Element(1), pl.Element(t), pl.Element(width))
        return pl.BlockSpec(shape, lambda b, g, i: (b, pl.multiple_of(base + i * t, LANE),
                                                    pl.multiple_of(g * width, LANE)))

    return pl.pallas_call(
        _attn_kernel,
        grid=(batch, MLA_HEADS // HEAD_GROUP, SEQ // t),
        in_specs=[frames(gw), frames(gw // 2),
                  pl.BlockSpec((1, base, gw), lambda b, g, i: (b, 0, g)),
                  pl.BlockSpec((1, base, gw // 2), lambda b, g, i: (b, 0, g)),
                  pl.BlockSpec((1, TP, gw), lambda b, g, i: (b, 0, g)),
                  pl.BlockSpec((1, TP, 2 * LANE), lambda b, g, i: (b, 0, 0)),
                  pl.BlockSpec((gw, TP), lambda b, g, i: (g, b))],
        out_specs=pl.BlockSpec((1, TP, gw), lambda b, g, i: (b, 0, g)),
        out_shape=jax.ShapeDtypeStruct((batch, TP, MLA_HEADS * V_HEAD), BF16),
        scratch_shapes=[pltpu.VMEM((2, n_chain, t, MXU_DIM), F32),
                        pltpu.VMEM((2, n_chain, 1, MXU_DIM), F32),
                        pltpu.VMEM((n_chain, 1, MXU_DIM), F32),
                        pltpu.VMEM((n_chain, V_HEAD + L_ROWS, MXU_DIM), F32)],
        compiler_params=_params("parallel", "parallel", "arbitrary"),
        name="mla_attention",
    )(qn, qr, qn, qr, kn, kr, vt)


def _proj_residual_kernel(h_ref, a_ref, w_ref, o_ref):
    o_ref[...] = h_ref[...] + _dot(a_ref[...], w_ref[...])


def _proj_residual(h, a, w, layer):
    batch = h.shape[0]
    rows = batch * TP
    tm = PROJ_ROW_TILE
    blk = lambda width: pl.BlockSpec((tm, width), lambda i: (i, 0))
    return pl.pallas_call(
        _proj_residual_kernel,
        grid=(rows // tm,),
        in_specs=[blk(D_MODEL), blk(a.shape[2]), _layer_spec(w.shape[1:], layer)],
        out_specs=blk(D_MODEL),
        out_shape=jax.ShapeDtypeStruct((rows, D_MODEL), F32),
        compiler_params=_params("parallel"),
        name="proj_residual",
    )(h.reshape(rows, D_MODEL), a.reshape(rows, a.shape[2]), w).reshape(h.shape)


def _ffn_kernel(h_ref, g_ref, wg_ref, wu_ref, wd_ref, o_ref, hn_sc):
    def swiglu_tile(hn):
        gate = _dot(hn, wg_ref[...])
        up = _dot(hn, wu_ref[...])
        act = (gate * jax.nn.sigmoid(gate) * up).astype(BF16)
        return _dot(act, wd_ref[...])

    @pl.when(pl.program_id(1) == 0)
    def _():
        half = FFN_ROW_TILE // 2
        for r in range(2):
            rows = slice(r * half, (r + 1) * half)
            x = h_ref[rows, :]
            hn = _rms(x, g_ref[...]).astype(BF16)
            hn_sc[rows, :] = hn
            o_ref[rows, :] = x + swiglu_tile(hn)

    @pl.when(pl.program_id(1) > 0)
    def _():
        half = FFN_ROW_TILE // 2
        for r in range(2):
            rows = slice(r * half, (r + 1) * half)
            o_ref[rows, :] += swiglu_tile(hn_sc[rows, :])


def _ffn(h, g, w_gu, w_down, layer):
    rows = h.shape[0]
    tm = FFN_ROW_TILE
    n_f = D_FF // FF_TILE
    return pl.pallas_call(
        _ffn_kernel,
        grid=(rows // tm, n_f),
        in_specs=[pl.BlockSpec((tm, D_MODEL), lambda i, j: (i, 0)),
                  pl.BlockSpec((1, D_MODEL), lambda i, j: (0, 0)),
                  pl.BlockSpec((None, D_MODEL, FF_TILE), lambda i, j: (layer, 0, j)),
                  pl.BlockSpec((None, D_MODEL, FF_TILE), lambda i, j: (layer, 0, j + n_f)),
                  pl.BlockSpec((None, FF_TILE, D_MODEL), lambda i, j: (layer, j, 0))],
        out_specs=pl.BlockSpec((tm, D_MODEL), lambda i, j: (i, 0)),
        out_shape=jax.ShapeDtypeStruct(h.shape, F32),
        scratch_shapes=[pltpu.VMEM((tm, D_MODEL), BF16)],
        compiler_params=_params("parallel", "arbitrary"),
        name="ffn",
    )(h, g, w_gu, w_gu, w_down)


def _lru_in_kernel(h_ref, g_ref, w_ref, xb_ref, yg_ref):
    half = LRU_IN_TILE // 2
    for r in range(2):
        rows = slice(r * half, (r + 1) * half)
        hn = _rms(h_ref[rows, :], g_ref[...]).astype(BF16)
        xb_ref[rows, :] = _dot(hn, w_ref[:, :D_RNN])
        y = _dot(hn, w_ref[:, D_RNN:])
        c = (2.0 / jnp.pi) ** 0.5
        yg_ref[rows, :] = 0.5 * y * (1.0 + jnp.tanh(c * (y + 0.044715 * (y * y * y))))


def _lru_in(h, g, w, layer):
    rows = h.shape[0]
    tm = LRU_IN_TILE
    row = lambda width: pl.BlockSpec((tm, width), lambda i: (i, 0))
    return pl.pallas_call(
        _lru_in_kernel,
        grid=(rows // tm,),
        in_specs=[row(D_MODEL), _const_spec(g.shape), _layer_spec(w.shape[1:], layer)],
        out_specs=[row(D_RNN), row(D_RNN)],
        out_shape=[jax.ShapeDtypeStruct((rows, D_RNN), F32)] * 2,
        compiler_params=_params("parallel"),
        name="lru_in",
    )(h, g, w)


def _lru_kernel(xb_ref, yg_ref, cw_ref, cb_ref, wga_ref, bga_ref, wgx_ref, bgx_ref, lam_ref,
                o_ref, x_sc, tail_sc, a_sc, b_sc, hs_sc, h_sc):
    t = pl.program_id(1)
    tt = SEQ_TILE
    halo = 8

    x_sc[...] = xb_ref[0]

    @pl.when(t == 0)
    def _():
        x_sc[0:PAD, :] = jnp.zeros((PAD, D_RNN), F32)
        tail_sc[...] = jnp.zeros(tail_sc.shape, F32)
        h_sc[...] = jnp.zeros(h_sc.shape, F32)

    neg_lam = -lam_ref[...]
    softplus = jnp.maximum(neg_lam, 0.0) + jnp.log1p(jnp.exp(-jnp.abs(neg_lam)))
    neg_log_a_rate = LRU_C * softplus
    a_exp2_rate = -math.log2(math.e) * neg_log_a_rate
    for g in range(D_RNN // GATE_TILE):
        cols = slice(g * GATE_TILE, (g + 1) * GATE_TILE)
        x = x_sc[:, cols]
        taps = [cw_ref[k:k + 1, cols] for k in range(CONV_W)]
        xc = cb_ref[:, cols] + taps[CONV_W - 1] * x
        for d in range(1, CONV_W):
            xc = xc + taps[CONV_W - 1 - d] * pltpu.roll(x, d, axis=0)
        head = jnp.concatenate([tail_sc[:, cols], x[0:halo]], axis=0)
        xc_head = cb_ref[:, cols]
        for k in range(CONV_W):
            o = halo - (CONV_W - 1) + k
            xc_head = xc_head + taps[k] * head[o:o + halo]
        xc = jnp.concatenate([xc_head, xc[halo:]], axis=0)
        xc16 = xc.astype(BF16)
        r = jax.nn.sigmoid(_dot(xc16, wga_ref[g]) + bga_ref[:, cols])
        ig = jax.nn.sigmoid(_dot(xc16, wgx_ref[g]) + bgx_ref[:, cols])
        a = jnp.exp2(r * a_exp2_rate[:, cols])
        a_sc[:, :, cols] = a.reshape(tt // SCAN_BLOCK, SCAN_BLOCK, GATE_TILE)
        gain = jnp.sqrt(jnp.tanh(r * neg_log_a_rate[:, cols]) * (1.0 + a * a))
        b_sc[:, :, cols] = (gain * (ig * xc)).reshape(tt // SCAN_BLOCK, SCAN_BLOCK, GATE_TILE)

    tail_sc[...] = x_sc[tt - halo:tt, :]

    @pl.when(t == 0)
    def _():
        b_sc[0:PAD // SCAN_BLOCK] = jnp.zeros((PAD // SCAN_BLOCK, SCAN_BLOCK, D_RNN), F32)

    def body(blk, h):
        a_cum = b_cum = h_out = None
        for k in range(SCAN_BLOCK):
            a_k = a_sc[blk, k:k + 1, :]
            b_k = b_sc[blk, k:k + 1, :]
            if k == 0:
                a_cum, b_cum = a_k, b_k
            else:
                a_cum, b_cum = a_k * a_cum, a_k * b_cum + b_k
            h_out = a_cum * h + b_cum
            hs_sc[blk, k:k + 1, :] = h_out
        return h_out

    h_sc[...] = lax.fori_loop(0, tt // SCAN_BLOCK, body, h_sc[...], unroll=4)
    o_ref[0] = (hs_sc[...].reshape(tt, D_RNN) * yg_ref[0]).astype(BF16)


def _lru(xb, yg, cw, cb, wga, bga, wgx, bgx, lam):
    batch = xb.shape[0]
    tt = SEQ_TILE
    blk = pl.BlockSpec((1, tt, D_RNN), lambda b, t: (b, t, 0))
    consts = (cw, cb, wga, bga, wgx, bgx, lam)
    return pl.pallas_call(
        _lru_kernel,
        grid=(batch, TP // tt),
        in_specs=[blk, blk] + [_const_spec(c.shape) for c in consts],
        out_specs=blk,
        out_shape=jax.ShapeDtypeStruct(xb.shape, BF16),
        scratch_shapes=[pltpu.VMEM((tt, D_RNN), F32), pltpu.VMEM((8, D_RNN), F32)]
        + [pltpu.VMEM((tt // SCAN_BLOCK, SCAN_BLOCK, D_RNN), F32)] * 3
        + [pltpu.VMEM((1, D_RNN), F32)],
        compiler_params=_params("parallel", "arbitrary"),
        name="rglru",
    )(xb, yg, *consts)


def _final_norm_kernel(h_ref, g_ref, o_ref):
    o_ref[0] = _rms(h_ref[0], g_ref[...])


def _final_norm(h, g):
    batch = h.shape[0]
    return pl.pallas_call(
        _final_norm_kernel,
        grid=(batch, SEQ // ROW_TILE),
        in_specs=[pl.BlockSpec((pl.Element(1), pl.Element(ROW_TILE), pl.Element(D_MODEL)),
                               lambda b, i: (b, pl.multiple_of(PAD + N_META + i * ROW_TILE, LANE), 0)),
                  _const_spec(g.shape)],
        out_specs=pl.BlockSpec((1, ROW_TILE, D_MODEL), lambda b, i: (b, i, 0)),
        out_shape=jax.ShapeDtypeStruct((batch, SEQ, D_MODEL), F32),
        compiler_params=_params("parallel", "parallel"),
        name="final_norm",
    )(h, g)


def _rot_half_cols(w):
    return jnp.roll(w, -QK_ROPE // 2, axis=-1)


def _mla_weights(w_in, w_uq, w_ukv):
    w_q = w_in[:, :Q_LORA]
    w_kv = w_in[:, Q_LORA:Q_LORA + KV_LORA]
    w_kr = w_in[:, Q_LORA + KV_LORA:]
    z = jnp.zeros_like(w_kr)
    pair = lambda w: [w, z, z, w]
    win = jnp.concatenate([w_q, w_kv] + pair(w_kr) + pair(_rot_half_cols(w_kr)), axis=1)
    uq = w_uq.reshape(Q_LORA, MLA_HEADS, QK_NOPE + QK_ROPE)
    uq_n = uq[:, :, :QK_NOPE].reshape(Q_LORA, -1)
    uq_r = uq[:, :, QK_NOPE:]
    wuq = jnp.concatenate([uq_n, uq_r.reshape(Q_LORA, -1),
                           _rot_half_cols(uq_r).reshape(Q_LORA, -1)], axis=1)
    ukv = w_ukv.reshape(KV_LORA, MLA_HEADS, QK_NOPE + V_HEAD)
    wuk = ukv[:, :, :QK_NOPE].reshape(KV_LORA, -1)
    wvt = ukv[:, :, QK_NOPE:].reshape(KV_LORA, -1).T
    return win.astype(BF16), wuq.astype(BF16), wuk.astype(BF16), wvt.astype(BF16)


def _gate_block_diag(w):
    w = w.reshape(RNN_BLOCKS // 2, 2, RNN_BW, RNN_BW)
    z = jnp.zeros_like(w[:, 0])
    top = jnp.concatenate([w[:, 0], z], axis=2)
    bot = jnp.concatenate([z, w[:, 1]], axis=2)
    return jnp.concatenate([top, bot], axis=1).astype(BF16)


def _rope_tables():
    pos = jnp.maximum(jnp.arange(TP) - PAD, 0).astype(F32)
    inv_freq = ROPE_THETA ** (-jnp.arange(0, QK_ROPE, 2, dtype=F32) / QK_ROPE)
    ang = pos[:, None] * inv_freq[None, :]
    cos, sin = jnp.cos(ang), jnp.sin(ang)
    cos = jnp.concatenate([cos, cos, cos, cos], axis=1)
    sin = jnp.concatenate([-sin, sin, -sin, sin], axis=1)
    return cos, sin


def _mla_layer(h, g_mix, w_in, qg, kvg, w_uq, w_ukv, w_o_all, layer, cos, sin, meta=None):
    win, wuq, wuk, wvt = _mla_weights(w_in, w_uq, w_ukv)
    outs = _mla_proj(h, g_mix, win, qg, kvg, wuq, wuk, wvt, cos, sin, meta)
    if meta is not None:
        h = outs[5]
    return _proj_residual(h, _attention(*outs[:5]), w_o_all, layer)


def _lru_layer(h, g_mix, w_in_all, cw, cb, w_ga, b_ga, w_gx, b_gx, lam, w_o_all, layer):
    batch = h.shape[0]
    hf = h.reshape(batch * TP, D_MODEL)
    xb, yg = (a.reshape(batch, TP, D_RNN) for a in _lru_in(hf, g_mix, w_in_all, layer))
    mixed = _lru(xb, yg, cw, cb, _gate_block_diag(w_ga), b_ga, _gate_block_diag(w_gx), b_gx, lam)
    return _proj_residual(h, mixed, w_o_all, layer)


def kernel(x, meta_tokens, norm_mix, norm_ffn, norm_final, mla_w_in, mla_q_norm, mla_kv_norm, mla_w_uq, mla_w_ukv, mla_w_o, lru_w_in, lru_conv_w, lru_conv_b, lru_w_gate_a, lru_b_gate_a, lru_w_gate_x, lru_b_gate_x, lru_lambda, lru_w_o, ffn_w_gu, ffn_w_down):
    batch = x.shape[0]
    rows = batch * TP
    h = x
    cos, sin = _rope_tables()
    row_vec = lambda v: v.reshape(1, -1)
    flat = lambda a: a.reshape(rows, a.shape[-1])
    seq = lambda a: a.reshape(batch, TP, a.shape[-1])
    mla_w_o, lru_w_in, lru_w_o, ffn_w_gu, ffn_w_down = (
        w.astype(BF16) for w in (mla_w_o, lru_w_in, lru_w_o, ffn_w_gu, ffn_w_down))

    for layer in range(DEPTH):
        j = layer // N_MIXERS
        g_mix = row_vec(norm_mix[layer])
        if layer % N_MIXERS == 0:
            h = _mla_layer(h, g_mix, mla_w_in[j], row_vec(mla_q_norm[j]), row_vec(mla_kv_norm[j]),
                           mla_w_uq[j], mla_w_ukv[j], mla_w_o, j, cos, sin,
                           meta_tokens.astype(x.dtype) if layer == 0 else None)
        else:
            h = _lru_layer(h, g_mix, lru_w_in, lru_conv_w[j], row_vec(lru_conv_b[j]),
                           lru_w_gate_a[j], row_vec(lru_b_gate_a[j]), lru_w_gate_x[j],
                           row_vec(lru_b_gate_x[j]), row_vec(lru_lambda[j]), lru_w_o, j)
        h = seq(_ffn(flat(h), row_vec(norm_ffn[layer]), ffn_w_gu, ffn_w_down, layer))
    return _final_norm(h, row_vec(norm_final))
```
